```python
import math
import jax
import jax.numpy as jnp
from jax import lax
import numpy as np

D_MODEL = 1024
BATCH = 8
SEQ = 2048
DEPTH = 4
DEC_BATCH = 128
DEC_SEQ = 4
PAST_LEN = 2048
PAGE_SIZE = 128

N_EVEN = (DEPTH + 1) // 2
N_ODD = DEPTH // 2
HEAD_DIM = 64
Q_BLOCK = 128
EPS = 1e-6

A_HEADS = 4
A_DK = HEAD_DIM
A_DV = 2 * HEAD_DIM

B_HEADS = 8
B_GROUPS = 2
B_HPG = B_HEADS // B_GROUPS
L_CMP = 32
L_SEL = 64
N_SEL = 8
WINDOW = 512
FORCE_BONUS = 100.0

MIX_EVEN = A_HEADS * A_DV + B_HEADS * HEAD_DIM
EVEN_COLS = (A_HEADS * 2 * A_DK, A_HEADS * 2 * A_DK, A_HEADS * A_DV,
             B_HEADS * HEAD_DIM, 6 * B_GROUPS * HEAD_DIM, 3 * B_HEADS)
P_EVEN = 2 * A_HEADS * 2 * A_DK + A_HEADS * A_DV + B_HEADS * HEAD_DIM + 6 * B_GROUPS * HEAD_DIM + 3 * B_HEADS

C_HEADS = 16
MIX_ODD = C_HEADS * HEAD_DIM

P_HEADS = 8
N_KEYS = 128
N_EXPERTS = N_KEYS * N_KEYS
P_TOPK = 16
D_KEY = 256
PEER_CHUNK = 256

kernel_name = "hybrid_diff_nsa_stickbreak_peer_step"


def rmsnorm(x, g):
    xf = x.astype(jnp.float32)
    y = xf * lax.rsqrt(jnp.mean(xf * xf, axis=-1, keepdims=True) + EPS)
    return (y * g.astype(jnp.float32)).astype(x.dtype)


def alibi_slopes(n):
    return jnp.exp2(-8.0 * jnp.arange(1, n + 1, dtype=jnp.float32) / n)


def masked_softmax(s, mask):
    s = jnp.where(mask, s, -jnp.inf)
    m = jnp.max(s, axis=-1, keepdims=True)
    m = jnp.where(jnp.isfinite(m), m, 0.0)
    p = jnp.where(mask, jnp.exp(s - m), 0.0)
    return p / jnp.maximum(jnp.sum(p, axis=-1, keepdims=True), 1e-30)


def map_query_blocks(fn, xs):
    t = xs[0].shape[1]
    nb = t // Q_BLOCK

    def split(a):
        return jnp.moveaxis(a.reshape(a.shape[0], nb, Q_BLOCK, *a.shape[2:]), 1, 0)

    starts = jnp.arange(nb, dtype=jnp.int32) * Q_BLOCK
    out = lax.map(lambda a: fn(a[0], a[1]), (tuple(split(a) for a in xs), starts))
    out = jnp.moveaxis(out, 0, 1)
    return out.reshape(out.shape[0], t, *out.shape[3:])


def seg_einsum(eq, w, v_segs):
    out, off = None, 0
    for v in v_segs:
        n = v.shape[1]
        o = jnp.einsum(eq, w[..., off:off + n].astype(v.dtype), v, preferred_element_type=jnp.float32)
        out = o if out is None else out + o
        off += n
    return out


def lambda_full(lam_p, lam_init):
    lp = lam_p.astype(jnp.float32)
    return jnp.exp(jnp.sum(lp[0] * lp[1])) - jnp.exp(jnp.sum(lp[2] * lp[3])) + lam_init


def diff_attend(q, k_segs, v_segs, q_pos, k_pos, lam):
    s = jnp.concatenate([jnp.einsum('bqhmd,bkhmd->bhmqk', q, k, preferred_element_type=jnp.float32)
                         for k in k_segs], axis=-1) * (A_DK ** -0.5)
    dist = (q_pos[:, None] - k_pos[None, :]).astype(jnp.float32)
    s = s - alibi_slopes(A_HEADS)[None, :, None, None, None] * dist
    p = masked_softmax(s, dist >= 0)
    w = p[:, :, 0] - lam * p[:, :, 1]
    return seg_einsum('bhqk,bkhd->bqhd', w, v_segs).astype(q.dtype)


def nsa_compress(k, pe, w):
    b, l, g, d = k.shape
    kb = k.reshape(b, l // L_CMP, L_CMP, g, d) + pe[:, None, :]
    kb = jnp.swapaxes(kb, 2, 3).reshape(b, l // L_CMP, g, L_CMP * d)
    return jnp.einsum('bcgi,id->bcgd', kb, w)


def nsa_cmp_attend(q, kc, vc, q_pos):
    nc = kc.shape[1]
    blk_end = (jnp.arange(nc, dtype=jnp.int32) + 1) * L_CMP - 1
    dist = (q_pos[:, None] - blk_end[None, :]).astype(jnp.float32)
    slopes = alibi_slopes(B_HEADS).reshape(B_GROUPS, B_HPG)
    s = jnp.einsum('btgrd,bcgd->bgrtc', q, kc, preferred_element_type=jnp.float32) * (HEAD_DIM ** -0.5)
    s = s - slopes[None, :, :, None, None] * dist
    p = masked_softmax(s, dist >= 0)
    o = jnp.einsum('bgrtc,bcgd->btgrd', p.astype(vc.dtype), vc, preferred_element_type=jnp.float32)
    return o.astype(q.dtype), p


def nsa_select(p, q_pos):
    imp = jnp.sum(p, axis=2)
    b, g, t, nc = imp.shape
    per = L_SEL // L_CMP
    ns = nc // per
    imp = imp.reshape(b, g, t, ns, per).sum(-1)
    blk = jnp.arange(ns, dtype=jnp.int32)[None, :]
    cur = (q_pos // L_SEL)[:, None]
    valid = blk * L_SEL <= q_pos[:, None]
    forced = (blk == 0) | (blk == cur) | (blk == cur - 1)
    score = jnp.where(forced, imp + FORCE_BONUS, jnp.where(valid, imp, -1.0))
    _, idx = lax.top_k(score, min(N_SEL, ns))
    return idx


def to_sel_blocks(k):
    b, l, g, d = k.shape
    return jnp.transpose(k.reshape(b, l // L_SEL, L_SEL, g, d), (0, 3, 1, 2, 4))


def nsa_sel_attend(q, kb, vb, idx, q_pos):
    take = jax.vmap(jax.vmap(lambda blocks, ix: blocks[ix]))
    b, g, t, n = idx.shape
    kg = take(kb, idx).reshape(b, g, t, n * L_SEL, HEAD_DIM)
    vg = take(vb, idx).reshape(b, g, t, n * L_SEL, HEAD_DIM)
    kpos = (idx[..., None] * L_SEL + jnp.arange(L_SEL, dtype=jnp.int32)).reshape(b, g, t, n * L_SEL)
    dist = (q_pos[None, None, :, None] - kpos).astype(jnp.float32)[:, :, None]
    slopes = alibi_slopes(B_HEADS).reshape(B_GROUPS, B_HPG)
    s = jnp.einsum('btgrd,bgtkd->bgrtk', q, kg, preferred_element_type=jnp.float32) * (HEAD_DIM ** -0.5)
    s = s - slopes[None, :, :, None, None] * dist
    p = masked_softmax(s, dist >= 0)
    o = jnp.einsum('bgrtk,bgtkd->btgrd', p.astype(vg.dtype), vg, preferred_element_type=jnp.float32)
    return o.astype(q.dtype)


def nsa_win_attend(q, kw, vw, q_pos, k_pos):
    dist = q_pos[:, None] - k_pos[None, :]
    mask = (dist >= 0) & (dist <= WINDOW) & (k_pos[None, :] >= 0)
    slopes = alibi_slopes(B_HEADS).reshape(B_GROUPS, B_HPG)
    s = jnp.einsum('btgrd,bkgd->bgrtk', q, kw, preferred_element_type=jnp.float32) * (HEAD_DIM ** -0.5)
    s = s - slopes[None, :, :, None, None] * dist.astype(jnp.float32)
    p = masked_softmax(s, mask)
    o = jnp.einsum('bgrtk,bkgd->btgrd', p.astype(vw.dtype), vw, preferred_element_type=jnp.float32)
    return o.astype(q.dtype)


def nsa_prepare(q, kv, q_pos, pe, cw):
    l = kv.shape[1]
    lp = -(-l // L_SEL) * L_SEL
    kv = jnp.pad(kv, ((0, 0), (0, lp - l), (0, 0), (0, 0), (0, 0)))
    kc = nsa_compress(kv[:, :, 0], pe[0], cw[0])
    vc = nsa_compress(kv[:, :, 1], pe[1], cw[1])
    o_cmp, p = nsa_cmp_attend(q, kc, vc, q_pos)
    idx = nsa_select(p, q_pos)
    return o_cmp, idx, to_sel_blocks(kv[:, :, 2]), to_sel_blocks(kv[:, :, 3])


def even_project(h, w_in):
    b, t, _ = h.shape
    z = jnp.einsum('btd,dc->btc', h, w_in)
    cuts, acc = [], 0
    for c in EVEN_COLS[:-1]:
        acc += c
        cuts.append(acc)
    qa, ka, va, qb, kvb, gb = jnp.split(z, cuts, axis=-1)
    qa = qa.reshape(b, t, A_HEADS, 2, A_DK)
    ka = ka.reshape(b, t, A_HEADS, 2, A_DK)
    va = va.reshape(b, t, A_HEADS, A_DV)
    qb = qb.reshape(b, t, B_GROUPS, B_HPG, HEAD_DIM)
    kvb = kvb.reshape(b, t, 6, B_GROUPS, HEAD_DIM)
    gates = jax.nn.sigmoid(gb.reshape(b, t, B_GROUPS, B_HPG, 3))
    return qa, ka, va, qb, kvb, gates


def merge_even(o_a, o_b, subln, lam_init, w_out):
    b, t = o_a.shape[:2]
    o_a = (rmsnorm(o_a, subln) * (1.0 - lam_init)).reshape(b, t, -1)
    o = jnp.concatenate([o_a, o_b.reshape(b, t, -1)], axis=-1)
    return jnp.einsum('btc,cd->btd', o, w_out)


def even_mixer_prompt(h, w_in, w_out, lam_p, subln, pe, cw, lam_init):
    b, t, _ = h.shape
    qa, ka, va, qb, kvb, gates = even_project(h, w_in)
    lam = lambda_full(lam_p, lam_init)
    pos = jnp.arange(t, dtype=jnp.int32)
    blk_pos = jnp.arange(Q_BLOCK, dtype=jnp.int32)
    o_a = map_query_blocks(lambda xs, q0: diff_attend(xs[0], (ka,), (va,), q0 + blk_pos, pos, lam), (qa,))
    o_cmp, idx, kb, vb = nsa_prepare(qb, kvb[:, :, :4], pos, pe, cw)
    pad = ((0, 0), (WINDOW, 0), (0, 0), (0, 0))
    kw = jnp.pad(kvb[:, :, 4], pad)
    vw = jnp.pad(kvb[:, :, 5], pad)

    def sel_win_block(xs, q0):
        q_blk, idx_blk, g_blk = xs
        q_pos = q0 + blk_pos
        o_sel = nsa_sel_attend(q_blk, kb, vb, jnp.moveaxis(idx_blk, 1, 2), q_pos)
        k_pos = q0 - WINDOW + jnp.arange(WINDOW + Q_BLOCK, dtype=jnp.int32)
        kwb = lax.dynamic_slice_in_dim(kw, q0, WINDOW + Q_BLOCK, axis=1)
        vwb = lax.dynamic_slice_in_dim(vw, q0, WINDOW + Q_BLOCK, axis=1)
        o_win = nsa_win_attend(q_blk, kwb, vwb, q_pos, k_pos)
        return g_blk[..., 1:2] * o_sel + g_blk[..., 2:3] * o_win

    o_sw = map_query_blocks(sel_win_block, (qb, jnp.moveaxis(idx, 2, 1), gates))
    o_b = gates[..., 0:1] * o_cmp + o_sw
    y = merge_even(o_a, o_b, subln, lam_init, w_out)
    diff_rows = jnp.stack([ka.reshape(b, t, A_HEADS, 2 * A_DK), va], axis=2)
    win_state = kvb[:, t - min(WINDOW, t):, 4:]
    return y, diff_rows, kvb[:, :, :4], win_state


def even_mixer_sample(h, cache_diff, cache_nsa, win_state, page_table, j, w_in, w_out, lam_p, subln, pe, cw, lam_init):
    b, t, _ = h.shape
    past = page_table.shape[1] * PAGE_SIZE
    qa, ka, va, qb, kvb, gates = even_project(h, w_in)
    lam = lambda_full(lam_p, lam_init)
    q_pos = past + jnp.arange(t, dtype=jnp.int32)
    k_pos = jnp.arange(past + t, dtype=jnp.int32)
    k_past = cache_diff[j, page_table, :, 0].reshape(b, past, A_HEADS, 2, A_DK)
    v_past = cache_diff[j, page_table, :, 1].reshape(b, past, A_HEADS, A_DV)
    o_a = diff_attend(qa, (k_past, ka), (v_past, va), q_pos, k_pos, lam)
    nsa_past = cache_nsa[j, page_table].reshape(b, past, 4, B_GROUPS, HEAD_DIM)
    kv_all = jnp.concatenate([nsa_past, kvb[:, :, :4]], axis=1)
    o_cmp, idx, kb, vb = nsa_prepare(qb, kv_all, q_pos, pe, cw)
    o_sel = nsa_sel_attend(qb, kb, vb, idx, q_pos)
    wb = win_state.shape[1]
    win_all = jnp.concatenate([win_state, kvb[:, :, 4:]], axis=1)
    w_pos = past - wb + jnp.arange(wb + t, dtype=jnp.int32)
    o_win = nsa_win_attend(qb, win_all[:, :, 0], win_all[:, :, 1], q_pos, w_pos)
    o_b = gates[..., 0:1] * o_cmp + gates[..., 1:2] * o_sel + gates[..., 2:3] * o_win
    y = merge_even(o_a, o_b, subln, lam_init, w_out)
    diff_rows = jnp.stack([ka.reshape(b, t, A_HEADS, 2 * A_DK), va], axis=2)
    return y, diff_rows, kvb[:, :, :4], win_all[:, t:]


def sb_attend(q, k_segs, v_segs, q_pos, k_pos):
    z = jnp.concatenate([jnp.einsum('bqhd,bkhd->bhqk', q, k, preferred_element_type=jnp.float32)
                         for k in k_segs], axis=-1) * (HEAD_DIM ** -0.5)
    mask = k_pos[None, :] < q_pos[:, None]
    log_keep = jnp.where(mask, jax.nn.log_sigmoid(-z), 0.0)
    later = lax.cumsum(log_keep, axis=3, reverse=True) - log_keep
    a = jnp.where(mask, jnp.exp(jax.nn.log_sigmoid(z) + later), 0.0)
    return seg_einsum('bhqk,bkhd->bqhd', a, v_segs).astype(q.dtype)


def odd_project(h, w_in):
    b, t, _ = h.shape
    z = jnp.einsum('btd,dc->btc', h, w_in).reshape(b, t, 3, C_HEADS, HEAD_DIM)
    return z[:, :, 0], z[:, :, 1], z[:, :, 2]


def odd_mixer_prompt(h, w_in, w_out):
    b, t, _ = h.shape
    q, k, v = odd_project(h, w_in)
    pos = jnp.arange(t, dtype=jnp.int32)
    blk_pos = jnp.arange(Q_BLOCK, dtype=jnp.int32)
    o = map_query_blocks(lambda xs, q0: sb_attend(xs[0], (k,), (v,), q0 + blk_pos, pos), (q,))
    y = jnp.einsum('btc,cd->btd', o.reshape(b, t, MIX_ODD), w_out)
    return y, jnp.stack([k, v], axis=2)


def odd_mixer_sample(h, cache_sb, page_table, j, w_in, w_out):
    b, t, _ = h.shape
    past = page_table.shape[1] * PAGE_SIZE
    q, k, v = odd_project(h, w_in)
    k_past = cache_sb[j, page_table, :, 0].reshape(b, past, C_HEADS, HEAD_DIM)
    v_past = cache_sb[j, page_table, :, 1].reshape(b, past, C_HEADS, HEAD_DIM)
    q_pos = past + jnp.arange(t, dtype=jnp.int32)
    k_pos = jnp.arange(past + t, dtype=jnp.int32)
    o = sb_attend(q, (k_past, k), (v_past, v), q_pos, k_pos)
    y = jnp.einsum('btc,cd->btd', o.reshape(b, t, MIX_ODD), w_out)
    return y, jnp.stack([k, v], axis=2)


def peer_ffn(h, wq, subkeys, u, v):
    b, t, d = h.shape
    x = h.reshape(b * t, d)
    n = x.shape[0]
    x = jnp.pad(x, ((0, (-n) % PEER_CHUNK), (0, 0)))

    def chunk(xc):
        c = xc.shape[0]
        q = (xc @ wq).reshape(c, P_HEADS, 2, D_KEY // 2)
        s = jnp.einsum('chpd,hpkd->chpk', q, subkeys, preferred_element_type=jnp.float32)
        s1, i1 = lax.top_k(s[:, :, 0], P_TOPK)
        s2, i2 = lax.top_k(s[:, :, 1], P_TOPK)
        cand = (s1[..., :, None] + s2[..., None, :]).reshape(c, P_HEADS, P_TOPK * P_TOPK)
        cand_idx = (i1[..., :, None] * N_KEYS + i2[..., None, :]).reshape(c, P_HEADS, P_TOPK * P_TOPK)
        top, pos = lax.top_k(cand, P_TOPK)
        expert = jnp.take_along_axis(cand_idx, pos, axis=-1)
        gate = jax.nn.softmax(top, axis=-1)
        act = jax.nn.gelu(jnp.einsum('chkd,cd->chk', u[expert], xc, preferred_element_type=jnp.float32))
        coef = (gate * act).astype(xc.dtype)
        return jnp.einsum('chk,chkd->cd', coef, v[expert])

    y = lax.map(chunk, x.reshape(-1, PEER_CHUNK, d)).reshape(-1, d)[:n]
    return y.reshape(b, t, d)


def setup_inputs(seed: int = 0) -> dict:
    key = jax.random.key(seed)
    ks = jax.random.split(key, 24)
    f32 = jnp.float32
    n_pages = PAST_LEN // PAGE_SIZE
    n_used = DEC_BATCH * n_pages
    n_pool = n_used + max(1, n_used // 4)
    wb = min(WINDOW, PAST_LEN)

    def nrm(k, shape, scale=None):
        z = jax.random.normal(k, shape, f32)
        return z if scale is None else z * scale

    page_table = jax.random.permutation(ks[0], n_pool)[:n_used].reshape(DEC_BATCH, n_pages).astype(jnp.int32)
    return {
        "x_prompt": nrm(ks[1], (BATCH, SEQ, D_MODEL)),
        "x_sample": nrm(ks[2], (DEC_BATCH, DEC_SEQ, D_MODEL)),
        "cache_diff": nrm(ks[3], (N_EVEN, n_pool, PAGE_SIZE, 2, A_HEADS, A_DV)),
        "cache_nsa": nrm(ks[4], (N_EVEN, n_pool, PAGE_SIZE, 4, B_GROUPS, HEAD_DIM)),
        "state_nsa_win": nrm(ks[5], (N_EVEN, DEC_BATCH, wb, 2, B_GROUPS, HEAD_DIM)),
        "cache_sb": nrm(ks[6], (N_ODD, n_pool, PAGE_SIZE, 2, C_HEADS, HEAD_DIM)),
        "page_table": page_table,
        "norm_mix": 1.0 + nrm(ks[7], (DEPTH, D_MODEL), 0.02),
        "norm_ffn": 1.0 + nrm(ks[8], (DEPTH, D_MODEL), 0.02),
        "w_in_even": nrm(ks[9], (N_EVEN, D_MODEL, P_EVEN), D_MODEL ** -0.5),
        "w_out_even": nrm(ks[10], (N_EVEN, MIX_EVEN, D_MODEL), MIX_EVEN ** -0.5),
        "diff_lambda": nrm(ks[11], (N_EVEN, 4, A_DK), 0.1),
        "diff_subln": 1.0 + nrm(ks[12], (N_EVEN, A_DV), 0.02),
        "nsa_cmp_pe": nrm(ks[13], (N_EVEN, 2, L_CMP, HEAD_DIM), 0.02),
        "nsa_cmp_w": nrm(ks[14], (N_EVEN, 2, L_CMP * HEAD_DIM, HEAD_DIM), (L_CMP * HEAD_DIM) ** -0.5),
        "w_in_odd": nrm(ks[15], (N_ODD, D_MODEL, 3 * MIX_ODD), D_MODEL ** -0.5),
        "w_out_odd": nrm(ks[16], (N_ODD, MIX_ODD, D_MODEL), MIX_ODD ** -0.5),
        "peer_wq": nrm(ks[17], (DEPTH, D_MODEL, P_HEADS * D_KEY), D_MODEL ** -0.5),
        "peer_subkeys": nrm(ks[18], (DEPTH, P_HEADS, 2, N_KEYS, D_KEY // 2), (D_KEY // 2) ** -0.5),
        "peer_u": nrm(ks[19], (DEPTH, N_EXPERTS, D_MODEL), D_MODEL ** -0.5),
        "peer_v": nrm(ks[20], (DEPTH, N_EXPERTS, D_MODEL), 0.1),
        "norm_final": 1.0 + nrm(ks[21], (D_MODEL,), 0.02),
    }


def reference(x_prompt, x_sample, cache_diff, cache_nsa, state_nsa_win, cache_sb, page_table,
              norm_mix, norm_ffn, w_in_even, w_out_even, diff_lambda, diff_subln, nsa_cmp_pe, nsa_cmp_w,
              w_in_odd, w_out_odd, peer_wq, peer_subkeys, peer_u, peer_v, norm_final):
    yp, ys = x_prompt, x_sample
    diff_p, diff_s, nsa_p, nsa_s, win_p, win_s, sb_p, sb_s = [], [], [], [], [], [], [], []
    for i in range(DEPTH):
        j = i // 2
        hp = rmsnorm(yp, norm_mix[i])
        hs = rmsnorm(ys, norm_mix[i])
        if i % 2 == 0:
            lam_init = 0.8 - 0.6 * math.exp(-0.3 * i)
            mp, r_diff, r_nsa, r_win = even_mixer_prompt(
                hp, w_in_even[j], w_out_even[j], diff_lambda[j], diff_subln[j], nsa_cmp_pe[j], nsa_cmp_w[j], lam_init)
            ms, s_diff, s_nsa, s_win = even_mixer_sample(
                hs, cache_diff, cache_nsa, state_nsa_win[j], page_table, j,
                w_in_even[j], w_out_even[j], diff_lambda[j], diff_subln[j], nsa_cmp_pe[j], nsa_cmp_w[j], lam_init)
            diff_p.append(r_diff)
            diff_s.append(s_diff)
            nsa_p.append(r_nsa)
            nsa_s.append(s_nsa)
            win_p.append(r_win)
            win_s.append(s_win)
        else:
            mp, r_sb = odd_mixer_prompt(hp, w_in_odd[j], w_out_odd[j])
            ms, s_sb = odd_mixer_sample(hs, cache_sb, page_table, j, w_in_odd[j], w_out_odd[j])
            sb_p.append(r_sb)
            sb_s.append(s_sb)
        yp = yp + mp
        ys = ys + ms
        yp = yp + peer_ffn(rmsnorm(yp, norm_ffn[i]), peer_wq[i], peer_subkeys[i], peer_u[i], peer_v[i])
        ys = ys + peer_ffn(rmsnorm(ys, norm_ffn[i]), peer_wq[i], peer_subkeys[i], peer_u[i], peer_v[i])
    y_prompt = rmsnorm(yp, norm_final)
    y_sample = rmsnorm(ys, norm_final)
    return (y_prompt, y_sample, jnp.stack(diff_p), jnp.stack(diff_s), jnp.stack(nsa_p), jnp.stack(nsa_s),
            jnp.stack(win_p), jnp.stack(win_s), jnp.stack(sb_p), jnp.stack(sb_s))
```

```python
import math
import jax
import jax.numpy as jnp
from jax import lax
from jax.experimental import pallas as pl
from jax.experimental.pallas import tpu as pltpu

D_MODEL = 1024
DEPTH = 4
PAGE_SIZE = 128
HEAD_DIM = 64
Q_BLOCK = 128
EPS = 1e-6

A_HEADS = 4
A_DK = HEAD_DIM
A_DV = 2 * HEAD_DIM

B_HEADS = 8
B_GROUPS = 2
B_HPG = B_HEADS // B_GROUPS
L_CMP = 32
L_SEL = 64
N_SEL = 8
WINDOW = 512
FORCE_BONUS = 100.0

MIX_EVEN = A_HEADS * A_DV + B_HEADS * HEAD_DIM
EVEN_COLS = (A_HEADS * 2 * A_DK, A_HEADS * 2 * A_DK, A_HEADS * A_DV,
             B_HEADS * HEAD_DIM, 6 * B_GROUPS * HEAD_DIM, 3 * B_HEADS)

C_HEADS = 16
MIX_ODD = C_HEADS * HEAD_DIM

P_HEADS = 8
N_KEYS = 128
P_TOPK = 16
D_KEY = 256
PEER_CHUNK = 256


def rmsnorm(x, g):
    xf = x.astype(jnp.float32)
    y = xf * lax.rsqrt(jnp.mean(xf * xf, axis=-1, keepdims=True) + EPS)
    return (y * g.astype(jnp.float32)).astype(x.dtype)


def alibi_slopes(n):
    return jnp.exp2(-8.0 * jnp.arange(1, n + 1, dtype=jnp.float32) / n)


def masked_softmax(s, mask):
    s = jnp.where(mask, s, -jnp.inf)
    m = jnp.max(s, axis=-1, keepdims=True)
    m = jnp.where(jnp.isfinite(m), m, 0.0)
    p = jnp.where(mask, jnp.exp(s - m), 0.0)
    return p / jnp.maximum(jnp.sum(p, axis=-1, keepdims=True), 1e-30)


def map_query_blocks(fn, xs):
    t = xs[0].shape[1]
    nb = t // Q_BLOCK

    def split(a):
        return jnp.moveaxis(a.reshape(a.shape[0], nb, Q_BLOCK, *a.shape[2:]), 1, 0)

    starts = jnp.arange(nb, dtype=jnp.int32) * Q_BLOCK
    out = lax.map(lambda a: fn(a[0], a[1]), (tuple(split(a) for a in xs), starts))
    out = jnp.moveaxis(out, 0, 1)
    return out.reshape(out.shape[0], t, *out.shape[3:])


def seg_einsum(eq, w, v_segs):
    out, off = None, 0
    for v in v_segs:
        n = v.shape[1]
        o = jnp.einsum(eq, w[..., off:off + n].astype(v.dtype), v, preferred_element_type=jnp.float32)
        out = o if out is None else out + o
        off += n
    return out


def lambda_full(lam_p, lam_init):
    lp = lam_p.astype(jnp.float32)
    return jnp.exp(jnp.sum(lp[0] * lp[1])) - jnp.exp(jnp.sum(lp[2] * lp[3])) + lam_init


def diff_attend(q, k_segs, v_segs, q_pos, k_pos, lam):
    s = jnp.concatenate([jnp.einsum('bqhmd,bkhmd->bhmqk', q, k, preferred_element_type=jnp.float32)
                         for k in k_segs], axis=-1) * (A_DK ** -0.5)
    dist = (q_pos[:, None] - k_pos[None, :]).astype(jnp.float32)
    s = s - alibi_slopes(A_HEADS)[None, :, None, None, None] * dist
    p = masked_softmax(s, dist >= 0)
    w = p[:, :, 0] - lam * p[:, :, 1]
    return seg_einsum('bhqk,bkhd->bqhd', w, v_segs).astype(q.dtype)


def nsa_compress(k, pe, w):
    b, l, g, d = k.shape
    kb = k.reshape(b, l // L_CMP, L_CMP, g, d) + pe[:, None, :]
    kb = jnp.swapaxes(kb, 2, 3).reshape(b, l // L_CMP, g, L_CMP * d)
    return jnp.einsum('bcgi,id->bcgd', kb, w)


def nsa_cmp_attend(q, kc, vc, q_pos):
    nc = kc.shape[1]
    blk_end = (jnp.arange(nc, dtype=jnp.int32) + 1) * L_CMP - 1
    dist = (q_pos[:, None] - blk_end[None, :]).astype(jnp.float32)
    slopes = alibi_slopes(B_HEADS).reshape(B_GROUPS, B_HPG)
    s = jnp.einsum('btgrd,bcgd->bgrtc', q, kc, preferred_element_type=jnp.float32) * (HEAD_DIM ** -0.5)
    s = s - slopes[None, :, :, None, None] * dist
    p = masked_softmax(s, dist >= 0)
    o = jnp.einsum('bgrtc,bcgd->btgrd', p.astype(vc.dtype), vc, preferred_element_type=jnp.float32)
    return o.astype(q.dtype), p


def nsa_select(p, q_pos):
    imp = jnp.sum(p, axis=2)
    b, g, t, nc = imp.shape
    per = L_SEL // L_CMP
    ns = nc // per
    imp = imp.reshape(b, g, t, ns, per).sum(-1)
    blk = jnp.arange(ns, dtype=jnp.int32)[None, :]
    cur = (q_pos // L_SEL)[:, None]
    valid = blk * L_SEL <= q_pos[:, None]
    forced = (blk == 0) | (blk == cur) | (blk == cur - 1)
    score = jnp.where(forced, imp + FORCE_BONUS, jnp.where(valid, imp, -1.0))
    _, idx = lax.top_k(score, min(N_SEL, ns))
    return idx


def to_sel_blocks(k):
    b, l, g, d = k.shape
    return jnp.transpose(k.reshape(b, l // L_SEL, L_SEL, g, d), (0, 3, 1, 2, 4))


def nsa_sel_attend(q, kb, vb, idx, q_pos):
    take = jax.vmap(jax.vmap(lambda blocks, ix: blocks[ix]))
    b, g, t, n = idx.shape
    kg = take(kb, idx).reshape(b, g, t, n * L_SEL, HEAD_DIM)
    vg = take(vb, idx).reshape(b, g, t, n * L_SEL, HEAD_DIM)
    kpos = (idx[..., None] * L_SEL + jnp.arange(L_SEL, dtype=jnp.int32)).reshape(b, g, t, n * L_SEL)
    dist = (q_pos[None, None, :, None] - kpos).astype(jnp.float32)[:, :, None]
    slopes = alibi_slopes(B_HEADS).reshape(B_GROUPS, B_HPG)
    s = jnp.einsum('btgrd,bgtkd->bgrtk', q, kg, preferred_element_type=jnp.float32) * (HEAD_DIM ** -0.5)
    s = s - slopes[None, :, :, None, None] * dist
    p = masked_softmax(s, dist >= 0)
    o = jnp.einsum('bgrtk,bgtkd->btgrd', p.astype(vg.dtype), vg, preferred_element_type=jnp.float32)
    return o.astype(q.dtype)


def nsa_win_attend(q, kw, vw, q_pos, k_pos):
    dist = q_pos[:, None] - k_pos[None, :]
    mask = (dist >= 0) & (dist <= WINDOW) & (k_pos[None, :] >= 0)
    slopes = alibi_slopes(B_HEADS).reshape(B_GROUPS, B_HPG)
    s = jnp.einsum('btgrd,bkgd->bgrtk', q, kw, preferred_element_type=jnp.float32) * (HEAD_DIM ** -0.5)
    s = s - slopes[None, :, :, None, None] * dist.astype(jnp.float32)
    p = masked_softmax(s, mask)
    o = jnp.einsum('bgrtk,bkgd->btgrd', p.astype(vw.dtype), vw, preferred_element_type=jnp.float32)
    return o.astype(q.dtype)


def nsa_prepare(q, kv, q_pos, pe, cw):
    l = kv.shape[1]
    lp = -(-l // L_SEL) * L_SEL
    kv = jnp.pad(kv, ((0, 0), (0, lp - l), (0, 0), (0, 0), (0, 0)))
    kc = nsa_compress(kv[:, :, 0], pe[0], cw[0])
    vc = nsa_compress(kv[:, :, 1], pe[1], cw[1])
    o_cmp, p = nsa_cmp_attend(q, kc, vc, q_pos)
    idx = nsa_select(p, q_pos)
    return o_cmp, idx, to_sel_blocks(kv[:, :, 2]), to_sel_blocks(kv[:, :, 3])


def even_project(h, w_in):
    b, t, _ = h.shape
    z = jnp.einsum('btd,dc->btc', h, w_in)
    cuts, acc = [], 0
    for c in EVEN_COLS[:-1]:
        acc += c
        cuts.append(acc)
    qa, ka, va, qb, kvb, gb = jnp.split(z, cuts, axis=-1)
    qa = qa.reshape(b, t, A_HEADS, 2, A_DK)
    ka = ka.reshape(b, t, A_HEADS, 2, A_DK)
    va = va.reshape(b, t, A_HEADS, A_DV)
    qb = qb.reshape(b, t, B_GROUPS, B_HPG, HEAD_DIM)
    kvb = kvb.reshape(b, t, 6, B_GROUPS, HEAD_DIM)
    gates = jax.nn.sigmoid(gb.reshape(b, t, B_GROUPS, B_HPG, 3))
    return qa, ka, va, qb, kvb, gates


def merge_even(o_a, o_b, subln, lam_init, w_out):
    b, t = o_a.shape[:2]
    o_a = (rmsnorm(o_a, subln) * (1.0 - lam_init)).reshape(b, t, -1)
    o = jnp.concatenate([o_a, o_b.reshape(b, t, -1)], axis=-1)
    return jnp.einsum('btc,cd->btd', o, w_out)


def even_mixer_prompt(h, w_in, w_out, lam_p, subln, pe, cw, lam_init):
    b, t, _ = h.shape
    qa, ka, va, qb, kvb, gates = even_project(h, w_in)
    lam = lambda_full(lam_p, lam_init)
    pos = jnp.arange(t, dtype=jnp.int32)
    blk_pos = jnp.arange(Q_BLOCK, dtype=jnp.int32)
    o_a = map_query_blocks(lambda xs, q0: diff_attend(xs[0], (ka,), (va,), q0 + blk_pos, pos, lam), (qa,))
    o_cmp, idx, kb, vb = nsa_prepare(qb, kvb[:, :, :4], pos, pe, cw)
    pad = ((0, 0), (WINDOW, 0), (0, 0), (0, 0))
    kw = jnp.pad(kvb[:, :, 4], pad)
    vw = jnp.pad(kvb[:, :, 5], pad)

    def sel_win_block(xs, q0):
        q_blk, idx_blk, g_blk = xs
        q_pos = q0 + blk_pos
        o_sel = nsa_sel_attend(q_blk, kb, vb, jnp.moveaxis(idx_blk, 1, 2), q_pos)
        k_pos = q0 - WINDOW + jnp.arange(WINDOW + Q_BLOCK, dtype=jnp.int32)
        kwb = lax.dynamic_slice_in_dim(kw, q0, WINDOW + Q_BLOCK, axis=1)
        vwb = lax.dynamic_slice_in_dim(vw, q0, WINDOW + Q_BLOCK, axis=1)
        o_win = nsa_win_attend(q_blk, kwb, vwb, q_pos, k_pos)
        return g_blk[..., 1:2] * o_sel + g_blk[..., 2:3] * o_win

    o_sw = map_query_blocks(sel_win_block, (qb, jnp.moveaxis(idx, 2, 1), gates))
    o_b = gates[..., 0:1] * o_cmp + o_sw
    y = merge_even(o_a, o_b, subln, lam_init, w_out)
    diff_rows = jnp.stack([ka.reshape(b, t, A_HEADS, 2 * A_DK), va], axis=2)
    win_state = kvb[:, t - min(WINDOW, t):, 4:]
    return y, diff_rows, kvb[:, :, :4], win_state


def even_mixer_sample(h, cache_diff, cache_nsa, win_state, page_table, j, w_in, w_out, lam_p, subln, pe, cw, lam_init):
    b, t, _ = h.shape
    past = page_table.shape[1] * PAGE_SIZE
    qa, ka, va, qb, kvb, gates = even_project(h, w_in)
    lam = lambda_full(lam_p, lam_init)
    q_pos = past + jnp.arange(t, dtype=jnp.int32)
    k_pos = jnp.arange(past + t, dtype=jnp.int32)
    k_past = cache_diff[j, page_table, :, 0].reshape(b, past, A_HEADS, 2, A_DK)
    v_past = cache_diff[j, page_table, :, 1].reshape(b, past, A_HEADS, A_DV)
    o_a = diff_attend(qa, (k_past, ka), (v_past, va), q_pos, k_pos, lam)
    nsa_past = cache_nsa[j, page_table].reshape(b, past, 4, B_GROUPS, HEAD_DIM)
    kv_all = jnp.concatenate([nsa_past, kvb[:, :, :4]], axis=1)
    o_cmp, idx, kb, vb = nsa_prepare(qb, kv_all, q_pos, pe, cw)
    o_sel = nsa_sel_attend(qb, kb, vb, idx, q_pos)
    wb = win_state.shape[1]
    win_all = jnp.concatenate([win_state, kvb[:, :, 4:]], axis=1)
    w_pos = past - wb + jnp.arange(wb + t, dtype=jnp.int32)
    o_win = nsa_win_attend(qb, win_all[:, :, 0], win_all[:, :, 1], q_pos, w_pos)
    o_b = gates[..., 0:1] * o_cmp + gates[..., 1:2] * o_sel + gates[..., 2:3] * o_win
    y = merge_even(o_a, o_b, subln, lam_init, w_out)
    diff_rows = jnp.stack([ka.reshape(b, t, A_HEADS, 2 * A_DK), va], axis=2)
    return y, diff_rows, kvb[:, :, :4], win_all[:, t:]


def sb_attend(q, k_segs, v_segs, q_pos, k_pos):
    z = jnp.concatenate([jnp.einsum('bqhd,bkhd->bhqk', q, k, preferred_element_type=jnp.float32)
                         for k in k_segs], axis=-1) * (HEAD_DIM ** -0.5)
    mask = k_pos[None, :] < q_pos[:, None]
    log_keep = jnp.where(mask, jax.nn.log_sigmoid(-z), 0.0)
    later = lax.cumsum(log_keep, axis=3, reverse=True) - log_keep
    a = jnp.where(mask, jnp.exp(jax.nn.log_sigmoid(z) + later), 0.0)
    return seg_einsum('bhqk,bkhd->bqhd', a, v_segs).astype(q.dtype)


def odd_project(h, w_in):
    b, t, _ = h.shape
    z = jnp.einsum('btd,dc->btc', h, w_in).reshape(b, t, 3, C_HEADS, HEAD_DIM)
    return z[:, :, 0], z[:, :, 1], z[:, :, 2]


def odd_mixer_prompt(h, w_in, w_out):
    b, t, _ = h.shape
    q, k, v = odd_project(h, w_in)
    pos = jnp.arange(t, dtype=jnp.int32)
    blk_pos = jnp.arange(Q_BLOCK, dtype=jnp.int32)
    o = map_query_blocks(lambda xs, q0: sb_attend(xs[0], (k,), (v,), q0 + blk_pos, pos), (q,))
    y = jnp.einsum('btc,cd->btd', o.reshape(b, t, MIX_ODD), w_out)
    return y, jnp.stack([k, v], axis=2)


def odd_mixer_sample(h, cache_sb, page_table, j, w_in, w_out):
    b, t, _ = h.shape
    past = page_table.shape[1] * PAGE_SIZE
    q, k, v = odd_project(h, w_in)
    k_past = cache_sb[j, page_table, :, 0].reshape(b, past, C_HEADS, HEAD_DIM)
    v_past = cache_sb[j, page_table, :, 1].reshape(b, past, C_HEADS, HEAD_DIM)
    q_pos = past + jnp.arange(t, dtype=jnp.int32)
    k_pos = jnp.arange(past + t, dtype=jnp.int32)
    o = sb_attend(q, (k_past, k), (v_past, v), q_pos, k_pos)
    y = jnp.einsum('btc,cd->btd', o.reshape(b, t, MIX_ODD), w_out)
    return y, jnp.stack([k, v], axis=2)


def peer_ffn(h, wq, subkeys, u, v):
    b, t, d = h.shape
    x = h.reshape(b * t, d)
    n = x.shape[0]
    x = jnp.pad(x, ((0, (-n) % PEER_CHUNK), (0, 0)))

    def chunk(xc):
        c = xc.shape[0]
        q = (xc @ wq).reshape(c, P_HEADS, 2, D_KEY // 2)
        s = jnp.einsum('chpd,hpkd->chpk', q, subkeys, preferred_element_type=jnp.float32)
        s1, i1 = lax.top_k(s[:, :, 0], P_TOPK)
        s2, i2 = lax.top_k(s[:, :, 1], P_TOPK)
        cand = (s1[..., :, None] + s2[..., None, :]).reshape(c, P_HEADS, P_TOPK * P_TOPK)
        cand_idx = (i1[..., :, None] * N_KEYS + i2[..., None, :]).reshape(c, P_HEADS, P_TOPK * P_TOPK)
        top, pos = lax.top_k(cand, P_TOPK)
        expert = jnp.take_along_axis(cand_idx, pos, axis=-1)
        gate = jax.nn.softmax(top, axis=-1)
        act = jax.nn.gelu(jnp.einsum('chkd,cd->chk', u[expert], xc, preferred_element_type=jnp.float32))
        coef = (gate * act).astype(xc.dtype)
        return jnp.einsum('chk,chkd->cd', coef, v[expert])

    y = lax.map(chunk, x.reshape(-1, PEER_CHUNK, d)).reshape(-1, d)[:n]
    return y.reshape(b, t, d)


def _final_norm_body(x_ref, g_ref, o_ref):
    x = x_ref[...]
    o_ref[...] = x * lax.rsqrt(jnp.mean(x * x, axis=-1, keepdims=True) + EPS) * g_ref[...]


def _final_norm(x, g):
    b, t, d = x.shape
    n = b * t
    rows = 512
    out = pl.pallas_call(
        _final_norm_body,
        grid=(n // rows,),
        in_specs=[pl.BlockSpec((rows, d), lambda i: (i, 0)), pl.BlockSpec((1, d), lambda i: (0, 0))],
        out_specs=pl.BlockSpec((rows, d), lambda i: (i, 0)),
        out_shape=jax.ShapeDtypeStruct((n, d), x.dtype),
    )(x.reshape(n, d), g.reshape(1, d))
    return out.reshape(b, t, d)


def kernel(x_prompt, x_sample, cache_diff, cache_nsa, state_nsa_win, cache_sb, page_table,
           norm_mix, norm_ffn, w_in_even, w_out_even, diff_lambda, diff_subln, nsa_cmp_pe, nsa_cmp_w,
           w_in_odd, w_out_odd, peer_wq, peer_subkeys, peer_u, peer_v, norm_final):
    yp, ys = x_prompt, x_sample
    diff_p, diff_s, nsa_p, nsa_s, win_p, win_s, sb_p, sb_s = [], [], [], [], [], [], [], []
    for i in range(DEPTH):
        j = i // 2
        hp = rmsnorm(yp, norm_mix[i])
        hs = rmsnorm(ys, norm_mix[i])
        if i % 2 == 0:
            lam_init = 0.8 - 0.6 * math.exp(-0.3 * i)
            mp, r_diff, r_nsa, r_win = even_mixer_prompt(
                hp, w_in_even[j], w_out_even[j], diff_lambda[j], diff_subln[j], nsa_cmp_pe[j], nsa_cmp_w[j], lam_init)
            ms, s_diff, s_nsa, s_win = even_mixer_sample(
                hs, cache_diff, cache_nsa, state_nsa_win[j], page_table, j,
                w_in_even[j], w_out_even[j], diff_lambda[j], diff_subln[j], nsa_cmp_pe[j], nsa_cmp_w[j], lam_init)
            diff_p.append(r_diff)
            diff_s.append(s_diff)
            nsa_p.append(r_nsa)
            nsa_s.append(s_nsa)
            win_p.append(r_win)
            win_s.append(s_win)
        else:
            mp, r_sb = odd_mixer_prompt(hp, w_in_odd[j], w_out_odd[j])
            ms, s_sb = odd_mixer_sample(hs, cache_sb, page_table, j, w_in_odd[j], w_out_odd[j])
            sb_p.append(r_sb)
            sb_s.append(s_sb)
        yp = yp + mp
        ys = ys + ms
        yp = yp + peer_ffn(rmsnorm(yp, norm_ffn[i]), peer_wq[i], peer_subkeys[i], peer_u[i], peer_v[i])
        ys = ys + peer_ffn(rmsnorm(ys, norm_ffn[i]), peer_wq[i], peer_subkeys[i], peer_u[i], peer_v[i])
    y_prompt = _final_norm(yp, norm_final)
    y_sample = _final_norm(ys, norm_final)
    return (y_prompt, y_sample, jnp.stack(diff_p), jnp.stack(diff_s), jnp.stack(nsa_p), jnp.stack(nsa_s),
            jnp.stack(win_p), jnp.stack(win_s), jnp.stack(sb_p), jnp.stack(sb_s))
```

```python
import math
import jax
import jax.numpy as jnp
from jax import lax
from jax.experimental import pallas as pl
from jax.experimental.pallas import tpu as pltpu

D_MODEL = 1024
DEPTH = 4
PAGE_SIZE = 128
HEAD_DIM = 64
Q_BLOCK = 128
EPS = 1e-6

A_HEADS = 4
A_DK = HEAD_DIM
A_DV = 2 * HEAD_DIM

B_HEADS = 8
B_GROUPS = 2
B_HPG = B_HEADS // B_GROUPS
L_CMP = 32
L_SEL = 64
N_SEL = 8
WINDOW = 512
FORCE_BONUS = 100.0

MIX_EVEN = A_HEADS * A_DV + B_HEADS * HEAD_DIM
EVEN_COLS = (A_HEADS * 2 * A_DK, A_HEADS * 2 * A_DK, A_HEADS * A_DV,
             B_HEADS * HEAD_DIM, 6 * B_GROUPS * HEAD_DIM, 3 * B_HEADS)

C_HEADS = 16
MIX_ODD = C_HEADS * HEAD_DIM

P_HEADS = 8
N_KEYS = 128
P_TOPK = 16
D_KEY = 256
PEER_CHUNK = 256


def rmsnorm(x, g):
    xf = x.astype(jnp.float32)
    y = xf * lax.rsqrt(jnp.mean(xf * xf, axis=-1, keepdims=True) + EPS)
    return (y * g.astype(jnp.float32)).astype(x.dtype)


def alibi_slopes(n):
    return jnp.exp2(-8.0 * jnp.arange(1, n + 1, dtype=jnp.float32) / n)


def masked_softmax(s, mask):
    s = jnp.where(mask, s, -jnp.inf)
    m = jnp.max(s, axis=-1, keepdims=True)
    m = jnp.where(jnp.isfinite(m), m, 0.0)
    p = jnp.where(mask, jnp.exp(s - m), 0.0)
    return p / jnp.maximum(jnp.sum(p, axis=-1, keepdims=True), 1e-30)


def map_query_blocks(fn, xs):
    t = xs[0].shape[1]
    nb = t // Q_BLOCK

    def split(a):
        return jnp.moveaxis(a.reshape(a.shape[0], nb, Q_BLOCK, *a.shape[2:]), 1, 0)

    starts = jnp.arange(nb, dtype=jnp.int32) * Q_BLOCK
    out = lax.map(lambda a: fn(a[0], a[1]), (tuple(split(a) for a in xs), starts))
    out = jnp.moveaxis(out, 0, 1)
    return out.reshape(out.shape[0], t, *out.shape[3:])


def seg_einsum(eq, w, v_segs):
    out, off = None, 0
    for v in v_segs:
        n = v.shape[1]
        o = jnp.einsum(eq, w[..., off:off + n].astype(v.dtype), v, preferred_element_type=jnp.float32)
        out = o if out is None else out + o
        off += n
    return out


def lambda_full(lam_p, lam_init):
    lp = lam_p.astype(jnp.float32)
    return jnp.exp(jnp.sum(lp[0] * lp[1])) - jnp.exp(jnp.sum(lp[2] * lp[3])) + lam_init


def diff_attend(q, k_segs, v_segs, q_pos, k_pos, lam):
    s = jnp.concatenate([jnp.einsum('bqhmd,bkhmd->bhmqk', q, k, preferred_element_type=jnp.float32)
                         for k in k_segs], axis=-1) * (A_DK ** -0.5)
    dist = (q_pos[:, None] - k_pos[None, :]).astype(jnp.float32)
    s = s - alibi_slopes(A_HEADS)[None, :, None, None, None] * dist
    p = masked_softmax(s, dist >= 0)
    w = p[:, :, 0] - lam * p[:, :, 1]
    return seg_einsum('bhqk,bkhd->bqhd', w, v_segs).astype(q.dtype)


def nsa_compress(k, pe, w):
    b, l, g, d = k.shape
    kb = k.reshape(b, l // L_CMP, L_CMP, g, d) + pe[:, None, :]
    kb = jnp.swapaxes(kb, 2, 3).reshape(b, l // L_CMP, g, L_CMP * d)
    return jnp.einsum('bcgi,id->bcgd', kb, w)


def nsa_cmp_attend(q, kc, vc, q_pos):
    nc = kc.shape[1]
    blk_end = (jnp.arange(nc, dtype=jnp.int32) + 1) * L_CMP - 1
    dist = (q_pos[:, None] - blk_end[None, :]).astype(jnp.float32)
    slopes = alibi_slopes(B_HEADS).reshape(B_GROUPS, B_HPG)
    s = jnp.einsum('btgrd,bcgd->bgrtc', q, kc, preferred_element_type=jnp.float32) * (HEAD_DIM ** -0.5)
    s = s - slopes[None, :, :, None, None] * dist
    p = masked_softmax(s, dist >= 0)
    o = jnp.einsum('bgrtc,bcgd->btgrd', p.astype(vc.dtype), vc, preferred_element_type=jnp.float32)
    return o.astype(q.dtype), p


def nsa_select(p, q_pos):
    imp = jnp.sum(p, axis=2)
    b, g, t, nc = imp.shape
    per = L_SEL // L_CMP
    ns = nc // per
    imp = imp.reshape(b, g, t, ns, per).sum(-1)
    blk = jnp.arange(ns, dtype=jnp.int32)[None, :]
    cur = (q_pos // L_SEL)[:, None]
    valid = blk * L_SEL <= q_pos[:, None]
    forced = (blk == 0) | (blk == cur) | (blk == cur - 1)
    score = jnp.where(forced, imp + FORCE_BONUS, jnp.where(valid, imp, -1.0))
    _, idx = lax.top_k(score, min(N_SEL, ns))
    return idx


def to_sel_blocks(k):
    b, l, g, d = k.shape
    return jnp.transpose(k.reshape(b, l // L_SEL, L_SEL, g, d), (0, 3, 1, 2, 4))


def nsa_sel_attend(q, kb, vb, idx, q_pos):
    take = jax.vmap(jax.vmap(lambda blocks, ix: blocks[ix]))
    b, g, t, n = idx.shape
    kg = take(kb, idx).reshape(b, g, t, n * L_SEL, HEAD_DIM)
    vg = take(vb, idx).reshape(b, g, t, n * L_SEL, HEAD_DIM)
    kpos = (idx[..., None] * L_SEL + jnp.arange(L_SEL, dtype=jnp.int32)).reshape(b, g, t, n * L_SEL)
    dist = (q_pos[None, None, :, None] - kpos).astype(jnp.float32)[:, :, None]
    slopes = alibi_slopes(B_HEADS).reshape(B_GROUPS, B_HPG)
    s = jnp.einsum('btgrd,bgtkd->bgrtk', q, kg, preferred_element_type=jnp.float32) * (HEAD_DIM ** -0.5)
    s = s - slopes[None, :, :, None, None] * dist
    p = masked_softmax(s, dist >= 0)
    o = jnp.einsum('bgrtk,bgtkd->btgrd', p.astype(vg.dtype), vg, preferred_element_type=jnp.float32)
    return o.astype(q.dtype)


def nsa_win_attend(q, kw, vw, q_pos, k_pos):
    dist = q_pos[:, None] - k_pos[None, :]
    mask = (dist >= 0) & (dist <= WINDOW) & (k_pos[None, :] >= 0)
    slopes = alibi_slopes(B_HEADS).reshape(B_GROUPS, B_HPG)
    s = jnp.einsum('btgrd,bkgd->bgrtk', q, kw, preferred_element_type=jnp.float32) * (HEAD_DIM ** -0.5)
    s = s - slopes[None, :, :, None, None] * dist.astype(jnp.float32)
    p = masked_softmax(s, mask)
    o = jnp.einsum('bgrtk,bkgd->btgrd', p.astype(vw.dtype), vw, preferred_element_type=jnp.float32)
    return o.astype(q.dtype)


def nsa_prepare(q, kv, q_pos, pe, cw):
    l = kv.shape[1]
    lp = -(-l // L_SEL) * L_SEL
    kv = jnp.pad(kv, ((0, 0), (0, lp - l), (0, 0), (0, 0), (0, 0)))
    kc = nsa_compress(kv[:, :, 0], pe[0], cw[0])
    vc = nsa_compress(kv[:, :, 1], pe[1], cw[1])
    o_cmp, p = nsa_cmp_attend(q, kc, vc, q_pos)
    idx = nsa_select(p, q_pos)
    return o_cmp, idx, to_sel_blocks(kv[:, :, 2]), to_sel_blocks(kv[:, :, 3])


def even_project(h, w_in):
    b, t, _ = h.shape
    z = jnp.einsum('btd,dc->btc', h, w_in)
    cuts, acc = [], 0
    for c in EVEN_COLS[:-1]:
        acc += c
        cuts.append(acc)
    qa, ka, va, qb, kvb, gb = jnp.split(z, cuts, axis=-1)
    qa = qa.reshape(b, t, A_HEADS, 2, A_DK)
    ka = ka.reshape(b, t, A_HEADS, 2, A_DK)
    va = va.reshape(b, t, A_HEADS, A_DV)
    qb = qb.reshape(b, t, B_GROUPS, B_HPG, HEAD_DIM)
    kvb = kvb.reshape(b, t, 6, B_GROUPS, HEAD_DIM)
    gates = jax.nn.sigmoid(gb.reshape(b, t, B_GROUPS, B_HPG, 3))
    return qa, ka, va, qb, kvb, gates


def merge_even(o_a, o_b, subln, lam_init, w_out):
    b, t = o_a.shape[:2]
    o_a = (rmsnorm(o_a, subln) * (1.0 - lam_init)).reshape(b, t, -1)
    o = jnp.concatenate([o_a, o_b.reshape(b, t, -1)], axis=-1)
    return jnp.einsum('btc,cd->btd', o, w_out)


def even_mixer_prompt(h, w_in, w_out, lam_p, subln, pe, cw, lam_init):
    b, t, _ = h.shape
    qa, ka, va, qb, kvb, gates = even_project(h, w_in)
    lam = lambda_full(lam_p, lam_init)
    pos = jnp.arange(t, dtype=jnp.int32)
    blk_pos = jnp.arange(Q_BLOCK, dtype=jnp.int32)
    o_a = map_query_blocks(lambda xs, q0: diff_attend(xs[0], (ka,), (va,), q0 + blk_pos, pos, lam), (qa,))
    o_cmp, idx, kb, vb = nsa_prepare(qb, kvb[:, :, :4], pos, pe, cw)
    pad = ((0, 0), (WINDOW, 0), (0, 0), (0, 0))
    kw = jnp.pad(kvb[:, :, 4], pad)
    vw = jnp.pad(kvb[:, :, 5], pad)

    def sel_win_block(xs, q0):
        q_blk, idx_blk, g_blk = xs
        q_pos = q0 + blk_pos
        o_sel = nsa_sel_attend(q_blk, kb, vb, jnp.moveaxis(idx_blk, 1, 2), q_pos)
        k_pos = q0 - WINDOW + jnp.arange(WINDOW + Q_BLOCK, dtype=jnp.int32)
        kwb = lax.dynamic_slice_in_dim(kw, q0, WINDOW + Q_BLOCK, axis=1)
        vwb = lax.dynamic_slice_in_dim(vw, q0, WINDOW + Q_BLOCK, axis=1)
        o_win = nsa_win_attend(q_blk, kwb, vwb, q_pos, k_pos)
        return g_blk[..., 1:2] * o_sel + g_blk[..., 2:3] * o_win

    o_sw = map_query_blocks(sel_win_block, (qb, jnp.moveaxis(idx, 2, 1), gates))
    o_b = gates[..., 0:1] * o_cmp + o_sw
    y = merge_even(o_a, o_b, subln, lam_init, w_out)
    diff_rows = jnp.stack([ka.reshape(b, t, A_HEADS, 2 * A_DK), va], axis=2)
    win_state = kvb[:, t - min(WINDOW, t):, 4:]
    return y, diff_rows, kvb[:, :, :4], win_state


def even_mixer_sample(h, cache_diff, cache_nsa, win_state, page_table, j, w_in, w_out, lam_p, subln, pe, cw, lam_init):
    b, t, _ = h.shape
    past = page_table.shape[1] * PAGE_SIZE
    qa, ka, va, qb, kvb, gates = even_project(h, w_in)
    lam = lambda_full(lam_p, lam_init)
    q_pos = past + jnp.arange(t, dtype=jnp.int32)
    k_pos = jnp.arange(past + t, dtype=jnp.int32)
    k_past = cache_diff[j, page_table, :, 0].reshape(b, past, A_HEADS, 2, A_DK)
    v_past = cache_diff[j, page_table, :, 1].reshape(b, past, A_HEADS, A_DV)
    o_a = diff_attend(qa, (k_past, ka), (v_past, va), q_pos, k_pos, lam)
    nsa_past = cache_nsa[j, page_table].reshape(b, past, 4, B_GROUPS, HEAD_DIM)
    kv_all = jnp.concatenate([nsa_past, kvb[:, :, :4]], axis=1)
    o_cmp, idx, kb, vb = nsa_prepare(qb, kv_all, q_pos, pe, cw)
    o_sel = nsa_sel_attend(qb, kb, vb, idx, q_pos)
    wb = win_state.shape[1]
    win_all = jnp.concatenate([win_state, kvb[:, :, 4:]], axis=1)
    w_pos = past - wb + jnp.arange(wb + t, dtype=jnp.int32)
    o_win = nsa_win_attend(qb, win_all[:, :, 0], win_all[:, :, 1], q_pos, w_pos)
    o_b = gates[..., 0:1] * o_cmp + gates[..., 1:2] * o_sel + gates[..., 2:3] * o_win
    y = merge_even(o_a, o_b, subln, lam_init, w_out)
    diff_rows = jnp.stack([ka.reshape(b, t, A_HEADS, 2 * A_DK), va], axis=2)
    return y, diff_rows, kvb[:, :, :4], win_all[:, t:]


def sb_attend(q, k_segs, v_segs, q_pos, k_pos):
    z = jnp.concatenate([jnp.einsum('bqhd,bkhd->bhqk', q, k, preferred_element_type=jnp.float32)
                         for k in k_segs], axis=-1) * (HEAD_DIM ** -0.5)
    mask = k_pos[None, :] < q_pos[:, None]
    log_keep = jnp.where(mask, jax.nn.log_sigmoid(-z), 0.0)
    later = lax.cumsum(log_keep, axis=3, reverse=True) - log_keep
    a = jnp.where(mask, jnp.exp(jax.nn.log_sigmoid(z) + later), 0.0)
    return seg_einsum('bhqk,bkhd->bqhd', a, v_segs).astype(q.dtype)


def odd_project(h, w_in):
    b, t, _ = h.shape
    z = jnp.einsum('btd,dc->btc', h, w_in).reshape(b, t, 3, C_HEADS, HEAD_DIM)
    return z[:, :, 0], z[:, :, 1], z[:, :, 2]


def odd_mixer_prompt(h, w_in, w_out):
    b, t, _ = h.shape
    q, k, v = odd_project(h, w_in)
    pos = jnp.arange(t, dtype=jnp.int32)
    blk_pos = jnp.arange(Q_BLOCK, dtype=jnp.int32)
    o = map_query_blocks(lambda xs, q0: sb_attend(xs[0], (k,), (v,), q0 + blk_pos, pos), (q,))
    y = jnp.einsum('btc,cd->btd', o.reshape(b, t, MIX_ODD), w_out)
    return y, jnp.stack([k, v], axis=2)


def odd_mixer_sample(h, cache_sb, page_table, j, w_in, w_out):
    b, t, _ = h.shape
    past = page_table.shape[1] * PAGE_SIZE
    q, k, v = odd_project(h, w_in)
    k_past = cache_sb[j, page_table, :, 0].reshape(b, past, C_HEADS, HEAD_DIM)
    v_past = cache_sb[j, page_table, :, 1].reshape(b, past, C_HEADS, HEAD_DIM)
    q_pos = past + jnp.arange(t, dtype=jnp.int32)
    k_pos = jnp.arange(past + t, dtype=jnp.int32)
    o = sb_attend(q, (k_past, k), (v_past, v), q_pos, k_pos)
    y = jnp.einsum('btc,cd->btd', o.reshape(b, t, MIX_ODD), w_out)
    return y, jnp.stack([k, v], axis=2)


LANES = 128
VMEM_LIMIT = 48 * 1024 * 1024
NEG_INF = float("-inf")
_NT = (((1,), (1,)), ((), ()))


def _peer_scores_body(y_ref, g_ref, wq_ref, sk_ref, xn_ref, st_ref):
    x = y_ref[...]
    xn = x * lax.rsqrt(jnp.mean(x * x, axis=-1, keepdims=True) + EPS) * g_ref[...]
    xb = xn.astype(jnp.bfloat16)
    xn_ref[...] = xb
    q = jnp.dot(xb, wq_ref[...], preferred_element_type=jnp.float32)
    half = D_KEY // 2
    for hp in range(2 * P_HEADS):
        qhp = q[:, hp * half:(hp + 1) * half].astype(jnp.bfloat16)
        st_ref[hp] = lax.dot_general(sk_ref[hp], qhp, _NT, preferred_element_type=jnp.float32)


def _pair_groups(a, b):
    rows = [a[r:r + 1] for r in range(8)]
    b_lo, b_hi = b[0:8], b[8:16]
    return [(rows[0], b_lo), (rows[0], b_hi)] + [(rows[r], b_lo) for r in range(1, 8)] + [(a[8:16], b[0:1])]


def _peer_gate_body(st_ref, e1_ref, e2_ref, gt_ref, a_scr, b_scr):
    def top_desc(s, scr):
        cur = s
        for r in range(P_TOPK):
            m = jnp.max(cur, axis=0, keepdims=True)
            scr[r:r + 1, :] = m
            cur = jnp.where(cur == m, NEG_INF, cur)
        return scr[...]

    for h in range(P_HEADS):
        s1 = st_ref[2 * h]
        s2 = st_ref[2 * h + 1]
        a = top_desc(s1, a_scr)
        b = top_desc(s2, b_scr)
        cand = [x + y for x, y in _pair_groups(a, b)]
        cur = cand
        tau = None
        for r in range(P_TOPK):
            mm = cur[0]
            for c in cur[1:]:
                mm = jnp.maximum(mm, c)
            tau = jnp.max(mm, axis=0, keepdims=True)
            if r + 1 < P_TOPK:
                cur = [jnp.where(c == tau, NEG_INF, c) for c in cur]
        sel = [c >= tau for c in cand]
        ea = jnp.exp(a - a[0:1])
        eb = jnp.exp(b - b[0:1])
        z = None
        for m, (x, y) in zip(sel, _pair_groups(ea, eb)):
            part = jnp.sum(jnp.where(m, x * y, 0.0), axis=0, keepdims=True)
            z = part if z is None else z + part
        inv_z = 1.0 / z
        gt = None
        for m, (x, y) in zip(sel, _pair_groups(ea * inv_z, eb)):
            part = jnp.min(jnp.where(m, x * y, jnp.inf), axis=0, keepdims=True)
            gt = part if gt is None else jnp.minimum(gt, part)
        e1_ref[h] = jnp.exp(s1 - a[0:1]) * inv_z
        e2_ref[h] = jnp.exp(s2 - b[0:1])
        gt_ref[h:h + 1, :] = gt


def _peer_main_body(y_ref, xn_ref, u_ref, vt_ref, e1_ref, e2_ref, gt_ref, o_ref, acc_ref, h_scr, c_scr):
    j = pl.program_id(1)
    te, tm = h_scr.shape

    @pl.when(j == 0)
    def _():
        acc_ref[...] = jnp.zeros_like(acc_ref)

    h_scr[...] = lax.dot_general(u_ref[...], xn_ref[...], _NT, preferred_element_type=jnp.float32)

    def lane_chunk(l, carry):
        lanes = pl.ds(pl.multiple_of(l * LANES, LANES), LANES)
        for c in range(te // N_KEYS):
            rows = slice(c * N_KEYS, (c + 1) * N_KEYS)
            gate = jnp.zeros((N_KEYS, LANES), jnp.float32)
            for h in range(P_HEADS):
                g = e1_ref[h, c:c + 1, lanes] * e2_ref[h, :, lanes]
                gate = gate + jnp.where(g >= gt_ref[h:h + 1, lanes], g, 0.0)
            act = jax.nn.gelu(h_scr[rows, lanes])
            c_scr[rows, lanes] = (gate * act).astype(jnp.bfloat16)
        return carry

    lax.fori_loop(0, tm // LANES, lane_chunk, 0)
    acc_ref[...] += jnp.dot(vt_ref[...], c_scr[...], preferred_element_type=jnp.float32)

    @pl.when(j == pl.num_programs(1) - 1)
    def _():
        o_ref[...] = y_ref[...] + acc_ref[...].T


def peer_residual(y, g, wq_b, sk_b, u_b, vt_b, tm=512, tm_gate=256, te=1024):
    n, d = y.shape
    n_exp = u_b.shape[0]
    hp2 = 2 * P_HEADS
    cparams = dict(vmem_limit_bytes=VMEM_LIMIT)
    xn, st = pl.pallas_call(
        _peer_scores_body,
        grid=(n // tm,),
        in_specs=[pl.BlockSpec((tm, d), lambda i: (i, 0)),
                  pl.BlockSpec((1, d), lambda i: (0, 0)),
                  pl.BlockSpec(wq_b.shape, lambda i: (0, 0)),
                  pl.BlockSpec(sk_b.shape, lambda i: (0, 0, 0))],
        out_specs=[pl.BlockSpec((tm, d), lambda i: (i, 0)),
                   pl.BlockSpec((hp2, N_KEYS, tm), lambda i: (0, 0, i))],
        out_shape=[jax.ShapeDtypeStruct((n, d), jnp.bfloat16),
                   jax.ShapeDtypeStruct((hp2, N_KEYS, n), jnp.float32)],
        compiler_params=pltpu.CompilerParams(dimension_semantics=("parallel",), **cparams),
        name="peer_scores",
    )(y, g.reshape(1, d), wq_b, sk_b)
    e1, e2, gt = pl.pallas_call(
        _peer_gate_body,
        grid=(n // tm_gate,),
        in_specs=[pl.BlockSpec((hp2, N_KEYS, tm_gate), lambda i: (0, 0, i))],
        out_specs=[pl.BlockSpec((P_HEADS, N_KEYS, tm_gate), lambda i: (0, 0, i)),
                   pl.BlockSpec((P_HEADS, N_KEYS, tm_gate), lambda i: (0, 0, i)),
                   pl.BlockSpec((P_HEADS, tm_gate), lambda i: (0, i))],
        out_shape=[jax.ShapeDtypeStruct((P_HEADS, N_KEYS, n), jnp.float32),
                   jax.ShapeDtypeStruct((P_HEADS, N_KEYS, n), jnp.float32),
                   jax.ShapeDtypeStruct((P_HEADS, n), jnp.float32)],
        scratch_shapes=[pltpu.VMEM((P_TOPK, tm_gate), jnp.float32), pltpu.VMEM((P_TOPK, tm_gate), jnp.float32)],
        compiler_params=pltpu.CompilerParams(dimension_semantics=("parallel",), **cparams),
        name="peer_gates",
    )(st)
    return pl.pallas_call(
        _peer_main_body,
        grid=(n // tm, n_exp // te),
        in_specs=[pl.BlockSpec((tm, d), lambda i, j: (i, 0)),
                  pl.BlockSpec((tm, d), lambda i, j: (i, 0)),
                  pl.BlockSpec((te, d), lambda i, j: (j, 0)),
                  pl.BlockSpec((d, te), lambda i, j: (0, j)),
                  pl.BlockSpec((P_HEADS, te // N_KEYS, tm), lambda i, j: (0, j, i)),
                  pl.BlockSpec((P_HEADS, N_KEYS, tm), lambda i, j: (0, 0, i)),
                  pl.BlockSpec((P_HEADS, tm), lambda i, j: (0, i))],
        out_specs=pl.BlockSpec((tm, d), lambda i, j: (i, 0)),
        out_shape=jax.ShapeDtypeStruct((n, d), jnp.float32),
        scratch_shapes=[pltpu.VMEM((d, tm), jnp.float32), pltpu.VMEM((te, tm), jnp.float32),
                        pltpu.VMEM((te, tm), jnp.bfloat16)],
        compiler_params=pltpu.CompilerParams(dimension_semantics=("parallel", "arbitrary"), **cparams),
        name="peer_main",
    )(y, xn, u_b, vt_b, e1, e2, gt)


def _final_norm_body(x_ref, g_ref, o_ref):
    x = x_ref[...]
    o_ref[...] = x * lax.rsqrt(jnp.mean(x * x, axis=-1, keepdims=True) + EPS) * g_ref[...]


def _final_norm(x, g, rows=512):
    n, d = x.shape
    return pl.pallas_call(
        _final_norm_body,
        grid=(n // rows,),
        in_specs=[pl.BlockSpec((rows, d), lambda i: (i, 0)), pl.BlockSpec((1, d), lambda i: (0, 0))],
        out_specs=pl.BlockSpec((rows, d), lambda i: (i, 0)),
        out_shape=jax.ShapeDtypeStruct((n, d), x.dtype),
        name="final_norm",
    )(x, g.reshape(1, d))


def kernel(x_prompt, x_sample, cache_diff, cache_nsa, state_nsa_win, cache_sb, page_table,
           norm_mix, norm_ffn, w_in_even, w_out_even, diff_lambda, diff_subln, nsa_cmp_pe, nsa_cmp_w,
           w_in_odd, w_out_odd, peer_wq, peer_subkeys, peer_u, peer_v, norm_final):
    bp, tp, d = x_prompt.shape
    bs, ts, _ = x_sample.shape
    n_p = bp * tp
    y_all = jnp.concatenate([x_prompt.reshape(n_p, d), x_sample.reshape(bs * ts, d)], axis=0)
    diff_p, diff_s, nsa_p, nsa_s, win_p, win_s, sb_p, sb_s = [], [], [], [], [], [], [], []
    bf16 = jnp.bfloat16
    for i in range(DEPTH):
        j = i // 2
        yp = y_all[:n_p].reshape(bp, tp, d)
        ys = y_all[n_p:].reshape(bs, ts, d)
        hp = rmsnorm(yp, norm_mix[i])
        hs = rmsnorm(ys, norm_mix[i])
        if i % 2 == 0:
            lam_init = 0.8 - 0.6 * math.exp(-0.3 * i)
            mp, r_diff, r_nsa, r_win = even_mixer_prompt(
                hp, w_in_even[j], w_out_even[j], diff_lambda[j], diff_subln[j], nsa_cmp_pe[j], nsa_cmp_w[j], lam_init)
            ms, s_diff, s_nsa, s_win = even_mixer_sample(
                hs, cache_diff, cache_nsa, state_nsa_win[j], page_table, j,
                w_in_even[j], w_out_even[j], diff_lambda[j], diff_subln[j], nsa_cmp_pe[j], nsa_cmp_w[j], lam_init)
            diff_p.append(r_diff)
            diff_s.append(s_diff)
            nsa_p.append(r_nsa)
            nsa_s.append(s_nsa)
            win_p.append(r_win)
            win_s.append(s_win)
        else:
            mp, r_sb = odd_mixer_prompt(hp, w_in_odd[j], w_out_odd[j])
            ms, s_sb = odd_mixer_sample(hs, cache_sb, page_table, j, w_in_odd[j], w_out_odd[j])
            sb_p.append(r_sb)
            sb_s.append(s_sb)
        y_all = y_all + jnp.concatenate([mp.reshape(n_p, d), ms.reshape(bs * ts, d)], axis=0)
        y_all = peer_residual(
            y_all, norm_ffn[i], peer_wq[i].astype(bf16),
            peer_subkeys[i].reshape(2 * P_HEADS, N_KEYS, D_KEY // 2).astype(bf16),
            peer_u[i].astype(bf16), peer_v[i].T.astype(bf16))
    y_out = _final_norm(y_all, norm_final)
    y_prompt = y_out[:n_p].reshape(bp, tp, d)
    y_sample = y_out[n_p:].reshape(bs, ts, d)
    return (y_prompt, y_sample, jnp.stack(diff_p), jnp.stack(diff_s), jnp.stack(nsa_p), jnp.stack(nsa_s),
            jnp.stack(win_p), jnp.stack(win_s), jnp.stack(sb_p), jnp.stack(sb_s))
```

```python
import math
import jax
import jax.numpy as jnp
from jax import lax
from jax.experimental import pallas as pl
from jax.experimental.pallas import tpu as pltpu

D_MODEL = 1024
DEPTH = 4
PAGE_SIZE = 128
HEAD_DIM = 64
Q_BLOCK = 128
EPS = 1e-6

A_HEADS = 4
A_DK = HEAD_DIM
A_DV = 2 * HEAD_DIM

B_HEADS = 8
B_GROUPS = 2
B_HPG = B_HEADS // B_GROUPS
L_CMP = 32
L_SEL = 64
N_SEL = 8
WINDOW = 512
FORCE_BONUS = 100.0

MIX_EVEN = A_HEADS * A_DV + B_HEADS * HEAD_DIM
EVEN_COLS = (A_HEADS * 2 * A_DK, A_HEADS * 2 * A_DK, A_HEADS * A_DV,
             B_HEADS * HEAD_DIM, 6 * B_GROUPS * HEAD_DIM, 3 * B_HEADS)

C_HEADS = 16
MIX_ODD = C_HEADS * HEAD_DIM

P_HEADS = 8
N_KEYS = 128
P_TOPK = 16
D_KEY = 256
PEER_CHUNK = 256


def rmsnorm(x, g):
    xf = x.astype(jnp.float32)
    y = xf * lax.rsqrt(jnp.mean(xf * xf, axis=-1, keepdims=True) + EPS)
    return (y * g.astype(jnp.float32)).astype(x.dtype)


def alibi_slopes(n):
    return jnp.exp2(-8.0 * jnp.arange(1, n + 1, dtype=jnp.float32) / n)


def masked_softmax(s, mask):
    s = jnp.where(mask, s, -jnp.inf)
    m = jnp.max(s, axis=-1, keepdims=True)
    m = jnp.where(jnp.isfinite(m), m, 0.0)
    p = jnp.where(mask, jnp.exp(s - m), 0.0)
    return p / jnp.maximum(jnp.sum(p, axis=-1, keepdims=True), 1e-30)


def map_query_blocks(fn, xs):
    t = xs[0].shape[1]
    nb = t // Q_BLOCK

    def split(a):
        return jnp.moveaxis(a.reshape(a.shape[0], nb, Q_BLOCK, *a.shape[2:]), 1, 0)

    starts = jnp.arange(nb, dtype=jnp.int32) * Q_BLOCK
    out = lax.map(lambda a: fn(a[0], a[1]), (tuple(split(a) for a in xs), starts))
    out = jnp.moveaxis(out, 0, 1)
    return out.reshape(out.shape[0], t, *out.shape[3:])


def seg_einsum(eq, w, v_segs):
    out, off = None, 0
    for v in v_segs:
        n = v.shape[1]
        o = jnp.einsum(eq, w[..., off:off + n].astype(v.dtype), v, preferred_element_type=jnp.float32)
        out = o if out is None else out + o
        off += n
    return out


def lambda_full(lam_p, lam_init):
    lp = lam_p.astype(jnp.float32)
    return jnp.exp(jnp.sum(lp[0] * lp[1])) - jnp.exp(jnp.sum(lp[2] * lp[3])) + lam_init


def diff_attend(q, k_segs, v_segs, q_pos, k_pos, lam):
    s = jnp.concatenate([jnp.einsum('bqhmd,bkhmd->bhmqk', q, k, preferred_element_type=jnp.float32)
                         for k in k_segs], axis=-1) * (A_DK ** -0.5)
    dist = (q_pos[:, None] - k_pos[None, :]).astype(jnp.float32)
    s = s - alibi_slopes(A_HEADS)[None, :, None, None, None] * dist
    p = masked_softmax(s, dist >= 0)
    w = p[:, :, 0] - lam * p[:, :, 1]
    return seg_einsum('bhqk,bkhd->bqhd', w, v_segs).astype(q.dtype)


def nsa_compress(k, pe, w):
    b, l, g, d = k.shape
    kb = k.reshape(b, l // L_CMP, L_CMP, g, d) + pe[:, None, :]
    kb = jnp.swapaxes(kb, 2, 3).reshape(b, l // L_CMP, g, L_CMP * d)
    return jnp.einsum('bcgi,id->bcgd', kb, w)


def nsa_cmp_attend(q, kc, vc, q_pos):
    nc = kc.shape[1]
    blk_end = (jnp.arange(nc, dtype=jnp.int32) + 1) * L_CMP - 1
    dist = (q_pos[:, None] - blk_end[None, :]).astype(jnp.float32)
    slopes = alibi_slopes(B_HEADS).reshape(B_GROUPS, B_HPG)
    s = jnp.einsum('btgrd,bcgd->bgrtc', q, kc, preferred_element_type=jnp.float32) * (HEAD_DIM ** -0.5)
    s = s - slopes[None, :, :, None, None] * dist
    p = masked_softmax(s, dist >= 0)
    o = jnp.einsum('bgrtc,bcgd->btgrd', p.astype(vc.dtype), vc, preferred_element_type=jnp.float32)
    return o.astype(q.dtype), p


def nsa_select(p, q_pos):
    imp = jnp.sum(p, axis=2)
    b, g, t, nc = imp.shape
    per = L_SEL // L_CMP
    ns = nc // per
    imp = imp.reshape(b, g, t, ns, per).sum(-1)
    blk = jnp.arange(ns, dtype=jnp.int32)[None, :]
    cur = (q_pos // L_SEL)[:, None]
    valid = blk * L_SEL <= q_pos[:, None]
    forced = (blk == 0) | (blk == cur) | (blk == cur - 1)
    score = jnp.where(forced, imp + FORCE_BONUS, jnp.where(valid, imp, -1.0))
    _, idx = lax.top_k(score, min(N_SEL, ns))
    return idx


def to_sel_blocks(k):
    b, l, g, d = k.shape
    return jnp.transpose(k.reshape(b, l // L_SEL, L_SEL, g, d), (0, 3, 1, 2, 4))


def nsa_sel_attend(q, kb, vb, idx, q_pos):
    take = jax.vmap(jax.vmap(lambda blocks, ix: blocks[ix]))
    b, g, t, n = idx.shape
    kg = take(kb, idx).reshape(b, g, t, n * L_SEL, HEAD_DIM)
    vg = take(vb, idx).reshape(b, g, t, n * L_SEL, HEAD_DIM)
    kpos = (idx[..., None] * L_SEL + jnp.arange(L_SEL, dtype=jnp.int32)).reshape(b, g, t, n * L_SEL)
    dist = (q_pos[None, None, :, None] - kpos).astype(jnp.float32)[:, :, None]
    slopes = alibi_slopes(B_HEADS).reshape(B_GROUPS, B_HPG)
    s = jnp.einsum('btgrd,bgtkd->bgrtk', q, kg, preferred_element_type=jnp.float32) * (HEAD_DIM ** -0.5)
    s = s - slopes[None, :, :, None, None] * dist
    p = masked_softmax(s, dist >= 0)
    o = jnp.einsum('bgrtk,bgtkd->btgrd', p.astype(vg.dtype), vg, preferred_element_type=jnp.float32)
    return o.astype(q.dtype)


def nsa_win_attend(q, kw, vw, q_pos, k_pos):
    dist = q_pos[:, None] - k_pos[None, :]
    mask = (dist >= 0) & (dist <= WINDOW) & (k_pos[None, :] >= 0)
    slopes = alibi_slopes(B_HEADS).reshape(B_GROUPS, B_HPG)
    s = jnp.einsum('btgrd,bkgd->bgrtk', q, kw, preferred_element_type=jnp.float32) * (HEAD_DIM ** -0.5)
    s = s - slopes[None, :, :, None, None] * dist.astype(jnp.float32)
    p = masked_softmax(s, mask)
    o = jnp.einsum('bgrtk,bkgd->btgrd', p.astype(vw.dtype), vw, preferred_element_type=jnp.float32)
    return o.astype(q.dtype)


def nsa_prepare(q, kv, q_pos, pe, cw):
    l = kv.shape[1]
    lp = -(-l // L_SEL) * L_SEL
    kv = jnp.pad(kv, ((0, 0), (0, lp - l), (0, 0), (0, 0), (0, 0)))
    kc = nsa_compress(kv[:, :, 0], pe[0], cw[0])
    vc = nsa_compress(kv[:, :, 1], pe[1], cw[1])
    o_cmp, p = nsa_cmp_attend(q, kc, vc, q_pos)
    idx = nsa_select(p, q_pos)
    return o_cmp, idx, to_sel_blocks(kv[:, :, 2]), to_sel_blocks(kv[:, :, 3])


def even_project(h, w_in):
    return even_split(jnp.einsum('btd,dc->btc', h, w_in))


def even_split(z):
    b, t, _ = z.shape
    cuts, acc = [], 0
    for c in EVEN_COLS[:-1]:
        acc += c
        cuts.append(acc)
    qa, ka, va, qb, kvb, gb = jnp.split(z, cuts, axis=-1)
    qa = qa.reshape(b, t, A_HEADS, 2, A_DK)
    ka = ka.reshape(b, t, A_HEADS, 2, A_DK)
    va = va.reshape(b, t, A_HEADS, A_DV)
    qb = qb.reshape(b, t, B_GROUPS, B_HPG, HEAD_DIM)
    kvb = kvb.reshape(b, t, 6, B_GROUPS, HEAD_DIM)
    gates = jax.nn.sigmoid(gb.reshape(b, t, B_GROUPS, B_HPG, 3))
    return qa, ka, va, qb, kvb, gates


def merge_even(o_a, o_b, subln, lam_init, w_out):
    b, t = o_a.shape[:2]
    o_a = (rmsnorm(o_a, subln) * (1.0 - lam_init)).reshape(b, t, -1)
    o = jnp.concatenate([o_a, o_b.reshape(b, t, -1)], axis=-1)
    return jnp.einsum('btc,cd->btd', o, w_out)


MASKED = -1e30


def _nsa_sw_body(q_ref, ks_ref, vs_ref, kw_ref, vw_ref, selm_ref, slope_ref, osel_ref, owin_ref, m_scr, l_scr, acc_scr):
    f32, bf16 = jnp.float32, jnp.bfloat16
    g = pl.program_id(1)
    qi = pl.program_id(2)
    tq = q_ref.shape[1]
    rows = B_HPG * tq
    lane = lax.broadcasted_iota(jnp.int32, (tq, 2 * HEAD_DIM), 1)
    mine = (lane >= g * HEAD_DIM) & (lane < (g + 1) * HEAD_DIM)
    q = q_ref[0] * (HEAD_DIM ** -0.5)
    parts = []
    for r in range(B_HPG):
        blk = q[:, 2 * HEAD_DIM * (r // 2):2 * HEAD_DIM * (r // 2 + 1)]
        src = jnp.where(g == r % 2, blk, pltpu.roll(blk, HEAD_DIM, axis=1))
        parts.append(jnp.where(mine, src, 0.0))
    qs = jnp.concatenate(parts, axis=0).astype(bf16)
    selm = selm_ref[0, 0].astype(bf16)
    selm = jnp.concatenate([selm] * B_HPG, axis=0)
    n_sel = selm.shape[1]
    blk_of_key = lax.broadcasted_iota(jnp.int32, (n_sel, tq), 1) // L_SEL
    blk_id = lax.broadcasted_iota(jnp.int32, (n_sel, tq), 0)
    q_pos = qi * tq + (lax.broadcasted_iota(jnp.int32, (rows, tq), 0) & (tq - 1))
    k_off = lax.broadcasted_iota(jnp.int32, (rows, tq), 1)
    slope = slope_ref[0]

    def attend(k_ref, v_ref, lo_tile, windowed):
        m_scr[...] = jnp.full_like(m_scr, MASKED)
        l_scr[...] = jnp.zeros_like(l_scr)
        acc_scr[...] = jnp.zeros_like(acc_scr)

        def key_tile(kt, carry):
            k0 = pl.multiple_of(kt * tq, tq)
            kb = k_ref[0, pl.ds(k0, tq), :].astype(bf16)
            vb = v_ref[0, pl.ds(k0, tq), :].astype(bf16)
            dist = q_pos - (k0 + k_off)
            s = lax.dot_general(qs, kb, _NT, preferred_element_type=f32) - slope * dist.astype(f32)
            if windowed:
                ok = (dist >= 0) & (dist <= WINDOW)
            else:
                expand = jnp.where(blk_id == kt * (tq // L_SEL) + blk_of_key, 1.0, 0.0).astype(bf16)
                chosen = jnp.dot(selm, expand, preferred_element_type=f32)
                ok = (dist >= 0) & (chosen > 0.5)
            s = jnp.where(ok, s, MASKED)
            m_old = m_scr[...]
            m_new = jnp.maximum(m_old, jnp.max(s, axis=1, keepdims=True))
            alpha = jnp.exp(m_old - m_new)
            p = jnp.where(ok, jnp.exp(s - m_new), 0.0)
            l_scr[...] = alpha * l_scr[...] + jnp.sum(p, axis=1, keepdims=True)
            acc_scr[...] = alpha * acc_scr[...] + jnp.dot(p.astype(bf16), vb, preferred_element_type=f32)
            m_scr[...] = m_new
            return carry

        lax.fori_loop(lo_tile, qi + 1, key_tile, 0)
        return acc_scr[...] / jnp.maximum(l_scr[...], 1e-30)

    def to_head_columns(o):
        blocks = []
        for j in range(B_HPG // 2):
            halves = []
            for hh in range(2):
                o_r = o[(2 * j + hh) * tq:(2 * j + hh + 1) * tq]
                halves.append(jnp.where(g == hh, o_r, pltpu.roll(o_r, HEAD_DIM, axis=1)))
            blocks.append(jnp.where(lane < HEAD_DIM, halves[0], halves[1]))
        return jnp.concatenate(blocks, axis=1)

    osel_ref[0] = to_head_columns(attend(ks_ref, vs_ref, 0, False))
    owin_ref[0] = to_head_columns(attend(kw_ref, vw_ref, jnp.maximum(qi - WINDOW // tq, 0), True))


def nsa_sel_win_prompt(z, selm, tq=Q_BLOCK):
    b, t, _ = z.shape
    blk = 2 * HEAD_DIM
    q_col = (EVEN_COLS[0] + EVEN_COLS[1] + EVEN_COLS[2]) // (B_HPG * HEAD_DIM)
    kv_col = (EVEN_COLS[0] + EVEN_COLS[1] + EVEN_COLS[2] + EVEN_COLS[3]) // blk
    slopes = alibi_slopes(B_HEADS).reshape(B_GROUPS, B_HPG, 1, 1)
    slopes = jnp.broadcast_to(slopes, (B_GROUPS, B_HPG, tq, blk)).reshape(B_GROUPS, B_HPG * tq, blk)
    kv_spec = lambda c: pl.BlockSpec((1, t, blk), lambda bi, g, qi: (bi, 0, kv_col + c))
    out_spec = pl.BlockSpec((1, tq, B_HPG * HEAD_DIM), lambda bi, g, qi: (bi, qi, g))
    out_sds = jax.ShapeDtypeStruct((b, t, B_HEADS * HEAD_DIM), jnp.float32)
    return pl.pallas_call(
        _nsa_sw_body,
        grid=(b, B_GROUPS, t // tq),
        in_specs=[pl.BlockSpec((1, tq, B_HPG * HEAD_DIM), lambda bi, g, qi: (bi, qi, q_col + g)),
                  kv_spec(2), kv_spec(3), kv_spec(4), kv_spec(5),
                  pl.BlockSpec((1, 1, tq, selm.shape[-1]), lambda bi, g, qi: (bi, g, qi, 0)),
                  pl.BlockSpec((1, B_HPG * tq, blk), lambda bi, g, qi: (g, 0, 0))],
        out_specs=[out_spec, out_spec],
        out_shape=[out_sds, out_sds],
        scratch_shapes=[pltpu.VMEM((B_HPG * tq, 1), jnp.float32), pltpu.VMEM((B_HPG * tq, 1), jnp.float32),
                        pltpu.VMEM((B_HPG * tq, blk), jnp.float32)],
        compiler_params=pltpu.CompilerParams(dimension_semantics=("parallel", "parallel", "arbitrary"),
                                             vmem_limit_bytes=VMEM_LIMIT),
        name="nsa_sel_win",
    )(z, z, z, z, z, selm, slopes)


DIFF_TK = 512


def _diff_prompt_body(q_ref, k_ref, v_ref, lam_ref, o_ref, m_scr, l_scr, acc_scr):
    f32, bf16 = jnp.float32, jnp.bfloat16
    qi = pl.program_id(2)
    tq = q_ref.shape[1]
    tk = DIFF_TK
    lane = lax.broadcasted_iota(jnp.int32, (tq, 2 * A_DK), 1)
    q = q_ref[0] * (A_DK ** -0.5)
    qs = jnp.concatenate([jnp.where(lane < A_DK, q, 0.0), jnp.where(lane >= A_DK, q, 0.0)], axis=0).astype(bf16)
    q_pos = qi * tq + (lax.broadcasted_iota(jnp.int32, (2 * tq, tk), 0) & (tq - 1))
    k_off = lax.broadcasted_iota(jnp.int32, (2 * tq, tk), 1)
    head = (pl.program_id(1) + 1).astype(f32)
    slope = jnp.exp2(jnp.zeros((1, tk), f32) - 8.0 * head / A_HEADS)
    m_scr[...] = jnp.full_like(m_scr, MASKED)
    l_scr[...] = jnp.zeros_like(l_scr)
    acc_scr[...] = jnp.zeros_like(acc_scr)

    def key_tile(kt, carry):
        k0 = pl.multiple_of(kt * tk, tk)
        kb = k_ref[0, pl.ds(k0, tk), :].astype(bf16)
        vb = v_ref[0, pl.ds(k0, tk), :].astype(bf16)
        dist = q_pos - (k0 + k_off)
        s = lax.dot_general(qs, kb, _NT, preferred_element_type=f32) - slope * dist.astype(f32)
        ok = dist >= 0
        s = jnp.where(ok, s, MASKED)
        m_old = m_scr[...]
        m_new = jnp.maximum(m_old, jnp.max(s, axis=1, keepdims=True))
        alpha = jnp.exp(m_old - m_new)
        p = jnp.where(ok, jnp.exp(s - m_new), 0.0)
        l_scr[...] = alpha * l_scr[...] + jnp.sum(p, axis=1, keepdims=True)
        acc_scr[...] = alpha * acc_scr[...] + jnp.dot(p.astype(bf16), vb, preferred_element_type=f32)
        m_scr[...] = m_new
        return carry

    lax.fori_loop(0, (qi * tq) // tk + 1, key_tile, 0)
    o = acc_scr[...] / jnp.maximum(l_scr[...], 1e-30)
    o_ref[0] = o[:tq] - lam_ref[...] * o[tq:]


def diff_prompt_attend(z, lam, tq=Q_BLOCK):
    b, t, _ = z.shape
    blk = 2 * A_DK
    lam_row = jnp.broadcast_to(jnp.asarray(lam, jnp.float32).reshape(1, 1), (1, A_DV))
    return pl.pallas_call(
        _diff_prompt_body,
        grid=(b, A_HEADS, t // tq),
        in_specs=[pl.BlockSpec((1, tq, blk), lambda bi, h, qi: (bi, qi, h)),
                  pl.BlockSpec((1, t, blk), lambda bi, h, qi: (bi, 0, A_HEADS + h)),
                  pl.BlockSpec((1, t, A_DV), lambda bi, h, qi: (bi, 0, 2 * A_HEADS + h)),
                  pl.BlockSpec((1, A_DV), lambda bi, h, qi: (0, 0))],
        out_specs=pl.BlockSpec((1, tq, A_DV), lambda bi, h, qi: (bi, qi, h)),
        out_shape=jax.ShapeDtypeStruct((b, t, A_HEADS * A_DV), jnp.float32),
        scratch_shapes=[pltpu.VMEM((2 * tq, 1), jnp.float32), pltpu.VMEM((2 * tq, 1), jnp.float32),
                        pltpu.VMEM((2 * tq, A_DV), jnp.float32)],
        compiler_params=pltpu.CompilerParams(dimension_semantics=("parallel", "parallel", "arbitrary"),
                                             vmem_limit_bytes=VMEM_LIMIT),
        name="diff_prompt",
    )(z, z, z, lam_row)


def even_mixer_prompt(h, w_in, w_out, lam_p, subln, pe, cw, lam_init):
    b, t, _ = h.shape
    z = jnp.einsum('btd,dc->btc', h, w_in)
    qa, ka, va, qb, kvb, gates = even_split(z)
    lam = lambda_full(lam_p, lam_init)
    pos = jnp.arange(t, dtype=jnp.int32)
    o_a = diff_prompt_attend(z, lam).reshape(b, t, A_HEADS, A_DV)
    o_cmp, idx, _, _ = nsa_prepare(qb, kvb[:, :, :4], pos, pe, cw)
    n_blk = t // L_SEL
    selm = jnp.any(idx[..., None] == jnp.arange(n_blk, dtype=jnp.int32), axis=-2).astype(jnp.float32)
    o_sel, o_win = nsa_sel_win_prompt(z, selm)
    heads = (b, t, B_GROUPS, B_HPG, HEAD_DIM)
    o_b = gates[..., 0:1] * o_cmp + gates[..., 1:2] * o_sel.reshape(heads) + gates[..., 2:3] * o_win.reshape(heads)
    y = merge_even(o_a, o_b, subln, lam_init, w_out)
    diff_rows = jnp.stack([ka.reshape(b, t, A_HEADS, 2 * A_DK), va], axis=2)
    win_state = kvb[:, t - min(WINDOW, t):, 4:]
    return y, diff_rows, kvb[:, :, :4], win_state


def even_mixer_sample(h, cache_diff, cache_nsa, win_state, page_table, j, w_in, w_out, lam_p, subln, pe, cw, lam_init):
    b, t, _ = h.shape
    past = page_table.shape[1] * PAGE_SIZE
    qa, ka, va, qb, kvb, gates = even_project(h, w_in)
    lam = lambda_full(lam_p, lam_init)
    q_pos = past + jnp.arange(t, dtype=jnp.int32)
    k_pos = jnp.arange(past + t, dtype=jnp.int32)
    k_past = cache_diff[j, page_table, :, 0].reshape(b, past, A_HEADS, 2, A_DK)
    v_past = cache_diff[j, page_table, :, 1].reshape(b, past, A_HEADS, A_DV)
    o_a = diff_attend(qa, (k_past, ka), (v_past, va), q_pos, k_pos, lam)
    nsa_past = cache_nsa[j, page_table].reshape(b, past, 4, B_GROUPS, HEAD_DIM)
    kv_all = jnp.concatenate([nsa_past, kvb[:, :, :4]], axis=1)
    o_cmp, idx, kb, vb = nsa_prepare(qb, kv_all, q_pos, pe, cw)
    o_sel = nsa_sel_attend(qb, kb, vb, idx, q_pos)
    wb = win_state.shape[1]
    win_all = jnp.concatenate([win_state, kvb[:, :, 4:]], axis=1)
    w_pos = past - wb + jnp.arange(wb + t, dtype=jnp.int32)
    o_win = nsa_win_attend(qb, win_all[:, :, 0], win_all[:, :, 1], q_pos, w_pos)
    o_b = gates[..., 0:1] * o_cmp + gates[..., 1:2] * o_sel + gates[..., 2:3] * o_win
    y = merge_even(o_a, o_b, subln, lam_init, w_out)
    diff_rows = jnp.stack([ka.reshape(b, t, A_HEADS, 2 * A_DK), va], axis=2)
    return y, diff_rows, kvb[:, :, :4], win_all[:, t:]


def sb_attend(q, k_segs, v_segs, q_pos, k_pos):
    z = jnp.concatenate([jnp.einsum('bqhd,bkhd->bhqk', q, k, preferred_element_type=jnp.float32)
                         for k in k_segs], axis=-1) * (HEAD_DIM ** -0.5)
    mask = k_pos[None, :] < q_pos[:, None]
    log_keep = jnp.where(mask, jax.nn.log_sigmoid(-z), 0.0)
    later = lax.cumsum(log_keep, axis=3, reverse=True) - log_keep
    a = jnp.where(mask, jnp.exp(jax.nn.log_sigmoid(z) + later), 0.0)
    return seg_einsum('bhqk,bkhd->bqhd', a, v_segs).astype(q.dtype)


def odd_project(h, w_in):
    b, t, _ = h.shape
    z = jnp.einsum('btd,dc->btc', h, w_in).reshape(b, t, 3, C_HEADS, HEAD_DIM)
    return z[:, :, 0], z[:, :, 1], z[:, :, 2]


SB_UNROLL = 4


def _sb_prompt_body(q_ref, k_ref, v_ref, o_ref):
    f32, bf16 = jnp.float32, jnp.bfloat16
    qi = pl.program_id(2)
    tq = q_ref.shape[1]
    row = lax.broadcasted_iota(jnp.int32, (tq, tq), 0)
    col = lax.broadcasted_iota(jnp.int32, (tq, tq), 1)
    lane = lax.broadcasted_iota(jnp.int32, (tq, 2 * HEAD_DIM), 1)
    later_keys = jnp.where(row > col, 1.0, 0.0).astype(bf16)
    q = q_ref[0] * (HEAD_DIM ** -0.5)
    qm = [jnp.where(lane < HEAD_DIM, q, 0.0).astype(bf16), jnp.where(lane >= HEAD_DIM, q, 0.0).astype(bf16)]

    def key_tiles(g, carry):
        carry = list(carry)
        chains = [(u, hh) for u in range(SB_UNROLL) for hh in range(2)]
        vbs, masks, zs = [], [], {}
        for u in range(SB_UNROLL):
            kt = qi - SB_UNROLL * g - u
            k0 = pl.multiple_of(jnp.maximum(kt, 0) * tq, tq)
            kb = k_ref[0, pl.ds(k0, tq), :].astype(bf16)
            vbs.append(v_ref[0, pl.ds(k0, tq), :].astype(bf16))
            masks.append(((kt * tq + col) < (qi * tq + row)) & (kt >= 0))
            for hh in range(2):
                zs[u, hh] = lax.dot_general(qm[hh], kb, _NT, preferred_element_type=f32)
        base, lks, later_in = {}, {}, {}
        for u, hh in chains:
            z = zs[u, hh]
            sp = jnp.maximum(z, 0.0) + jnp.log(1.0 + jnp.exp(-jnp.abs(z)))
            lks[u, hh] = jnp.where(masks[u], -sp, 0.0)
            base[u, hh] = z - sp
        for u, hh in chains:
            hi = lks[u, hh].astype(bf16)
            lo = (lks[u, hh] - hi.astype(f32)).astype(bf16)
            later_in[u, hh] = (jnp.dot(hi, later_keys, preferred_element_type=f32)
                               + jnp.dot(lo, later_keys, preferred_element_type=f32))
        probs = {}
        for u, hh in chains:
            c = carry[2 * hh]
            probs[u, hh] = jnp.where(masks[u], jnp.exp(base[u, hh] + later_in[u, hh] + c), 0.0).astype(bf16)
            carry[2 * hh] = c + jnp.sum(lks[u, hh], axis=1, keepdims=True)
        for u, hh in chains:
            carry[2 * hh + 1] = carry[2 * hh + 1] + jnp.dot(probs[u, hh], vbs[u], preferred_element_type=f32)
        return tuple(carry)

    zero_c, zero_acc = jnp.zeros((tq, 1), f32), jnp.zeros((tq, 2 * HEAD_DIM), f32)
    n_groups = (qi + SB_UNROLL) // SB_UNROLL
    out = lax.fori_loop(0, n_groups, key_tiles, (zero_c, zero_acc, zero_c, zero_acc))
    o_ref[0] = jnp.where(lane < HEAD_DIM, out[1], out[3])


def sb_prompt_attend(z, tq=Q_BLOCK):
    b, t, _ = z.shape
    pairs = MIX_ODD // (2 * HEAD_DIM)
    blk = 2 * HEAD_DIM
    return pl.pallas_call(
        _sb_prompt_body,
        grid=(b, pairs, t // tq),
        in_specs=[pl.BlockSpec((1, tq, blk), lambda bi, hp, qi: (bi, qi, hp)),
                  pl.BlockSpec((1, t, blk), lambda bi, hp, qi: (bi, 0, pairs + hp)),
                  pl.BlockSpec((1, t, blk), lambda bi, hp, qi: (bi, 0, 2 * pairs + hp))],
        out_specs=pl.BlockSpec((1, tq, blk), lambda bi, hp, qi: (bi, qi, hp)),
        out_shape=jax.ShapeDtypeStruct((b, t, MIX_ODD), jnp.float32),
        compiler_params=pltpu.CompilerParams(dimension_semantics=("parallel", "parallel", "arbitrary"),
                                             vmem_limit_bytes=VMEM_LIMIT),
        name="sb_prompt",
    )(z, z, z)


def odd_mixer_prompt(h, w_in, w_out):
    b, t, _ = h.shape
    z = jnp.einsum('btd,dc->btc', h, w_in)
    o = sb_prompt_attend(z)
    y = jnp.einsum('btc,cd->btd', o, w_out)
    kv = z[:, :, MIX_ODD:].reshape(b, t, 2, C_HEADS, HEAD_DIM)
    return y, kv


def odd_mixer_sample(h, cache_sb, page_table, j, w_in, w_out):
    b, t, _ = h.shape
    past = page_table.shape[1] * PAGE_SIZE
    q, k, v = odd_project(h, w_in)
    k_past = cache_sb[j, page_table, :, 0].reshape(b, past, C_HEADS, HEAD_DIM)
    v_past = cache_sb[j, page_table, :, 1].reshape(b, past, C_HEADS, HEAD_DIM)
    q_pos = past + jnp.arange(t, dtype=jnp.int32)
    k_pos = jnp.arange(past + t, dtype=jnp.int32)
    o = sb_attend(q, (k_past, k), (v_past, v), q_pos, k_pos)
    y = jnp.einsum('btc,cd->btd', o.reshape(b, t, MIX_ODD), w_out)
    return y, jnp.stack([k, v], axis=2)


LANES = 128
VMEM_LIMIT = 48 * 1024 * 1024
NEG_INF = float("-inf")
_NT = (((1,), (1,)), ((), ()))


def _peer_scores_body(y_ref, g_ref, wq_ref, sk_ref, xn_ref, st_ref):
    x = y_ref[...]
    xn = x * lax.rsqrt(jnp.mean(x * x, axis=-1, keepdims=True) + EPS) * g_ref[...]
    xb = xn.astype(jnp.bfloat16)
    xn_ref[...] = xb
    q = jnp.dot(xb, wq_ref[...], preferred_element_type=jnp.float32)
    half = D_KEY // 2
    for hp in range(2 * P_HEADS):
        qhp = q[:, hp * half:(hp + 1) * half].astype(jnp.bfloat16)
        st_ref[hp] = lax.dot_general(sk_ref[hp], qhp, _NT, preferred_element_type=jnp.float32)


def _pair_groups(a, b):
    rows = [a[r:r + 1] for r in range(8)]
    b_lo, b_hi = b[0:8], b[8:16]
    return [(rows[0], b_lo), (rows[0], b_hi)] + [(rows[r], b_lo) for r in range(1, 8)] + [(a[8:16], b[0:1])]


def _peer_gate_body(st_ref, e1_ref, e2_ref, gt_ref, a_scr, b_scr):
    def top_desc(s, scr):
        cur = s
        for r in range(P_TOPK):
            m = jnp.max(cur, axis=0, keepdims=True)
            scr[r:r + 1, :] = m
            cur = jnp.where(cur == m, NEG_INF, cur)
        return scr[...]

    for h in range(P_HEADS):
        s1 = st_ref[2 * h]
        s2 = st_ref[2 * h + 1]
        a = top_desc(s1, a_scr)
        b = top_desc(s2, b_scr)
        cand = [x + y for x, y in _pair_groups(a, b)]
        cur = cand
        tau = None
        for r in range(P_TOPK):
            mm = cur[0]
            for c in cur[1:]:
                mm = jnp.maximum(mm, c)
            tau = jnp.max(mm, axis=0, keepdims=True)
            if r + 1 < P_TOPK:
                cur = [jnp.where(c == tau, NEG_INF, c) for c in cur]
        sel = [c >= tau for c in cand]
        ea = jnp.exp(a - a[0:1])
        eb = jnp.exp(b - b[0:1])
        z = None
        for m, (x, y) in zip(sel, _pair_groups(ea, eb)):
            part = jnp.sum(jnp.where(m, x * y, 0.0), axis=0, keepdims=True)
            z = part if z is None else z + part
        inv_z = 1.0 / z
        gt = None
        for m, (x, y) in zip(sel, _pair_groups(ea * inv_z, eb)):
            part = jnp.min(jnp.where(m, x * y, jnp.inf), axis=0, keepdims=True)
            gt = part if gt is None else jnp.minimum(gt, part)
        e1_ref[h] = jnp.exp(s1 - a[0:1]) * inv_z
        e2_ref[h] = jnp.exp(s2 - b[0:1])
        gt_ref[h:h + 1, :] = gt


def _peer_main_body(y_ref, xn_ref, u_ref, vt_ref, e1_ref, e2_ref, gt_ref, o_ref, acc_ref, h_scr, c_scr):
    j = pl.program_id(1)
    te, tm = h_scr.shape

    @pl.when(j == 0)
    def _():
        acc_ref[...] = jnp.zeros_like(acc_ref)

    chunk = 2 * LANES
    chunks = [slice(t0, t0 + chunk) for t0 in range(0, tm, chunk)]
    for toks in chunks:
        h_scr[:, toks] = lax.dot_general(u_ref[...], xn_ref[toks, :], _NT, preferred_element_type=jnp.float32)
    for toks in chunks:
        for l0 in range(toks.start, toks.stop, LANES):
            lanes = slice(l0, l0 + LANES)
            for c in range(te // N_KEYS):
                rows = slice(c * N_KEYS, (c + 1) * N_KEYS)
                gate = jnp.zeros((N_KEYS, LANES), jnp.float32)
                for h in range(P_HEADS):
                    g = e1_ref[h, c:c + 1, lanes] * e2_ref[h, :, lanes]
                    gate = gate + jnp.where(g >= gt_ref[h:h + 1, lanes], g, 0.0)
                act = jax.nn.gelu(h_scr[rows, lanes])
                c_scr[rows, lanes] = (gate * act).astype(jnp.bfloat16)
        acc_ref[:, toks] += jnp.dot(vt_ref[...], c_scr[:, toks], preferred_element_type=jnp.float32)

    @pl.when(j == pl.num_programs(1) - 1)
    def _():
        o_ref[...] = y_ref[...] + acc_ref[...].T


def peer_residual(y, g, wq_b, sk_b, u_b, vt_b, tm=768, tm_gate=256, te=1024):
    n, d = y.shape
    n_exp = u_b.shape[0]
    hp2 = 2 * P_HEADS
    cparams = dict(vmem_limit_bytes=VMEM_LIMIT)
    xn, st = pl.pallas_call(
        _peer_scores_body,
        grid=(n // tm,),
        in_specs=[pl.BlockSpec((tm, d), lambda i: (i, 0)),
                  pl.BlockSpec((1, d), lambda i: (0, 0)),
                  pl.BlockSpec(wq_b.shape, lambda i: (0, 0)),
                  pl.BlockSpec(sk_b.shape, lambda i: (0, 0, 0))],
        out_specs=[pl.BlockSpec((tm, d), lambda i: (i, 0)),
                   pl.BlockSpec((hp2, N_KEYS, tm), lambda i: (0, 0, i))],
        out_shape=[jax.ShapeDtypeStruct((n, d), jnp.bfloat16),
                   jax.ShapeDtypeStruct((hp2, N_KEYS, n), jnp.float32)],
        compiler_params=pltpu.CompilerParams(dimension_semantics=("parallel",), **cparams),
        name="peer_scores",
    )(y, g.reshape(1, d), wq_b, sk_b)
    e1, e2, gt = pl.pallas_call(
        _peer_gate_body,
        grid=(n // tm_gate,),
        in_specs=[pl.BlockSpec((hp2, N_KEYS, tm_gate), lambda i: (0, 0, i))],
        out_specs=[pl.BlockSpec((P_HEADS, N_KEYS, tm_gate), lambda i: (0, 0, i)),
                   pl.BlockSpec((P_HEADS, N_KEYS, tm_gate), lambda i: (0, 0, i)),
                   pl.BlockSpec((P_HEADS, tm_gate), lambda i: (0, i))],
        out_shape=[jax.ShapeDtypeStruct((P_HEADS, N_KEYS, n), jnp.float32),
                   jax.ShapeDtypeStruct((P_HEADS, N_KEYS, n), jnp.float32),
                   jax.ShapeDtypeStruct((P_HEADS, n), jnp.float32)],
        scratch_shapes=[pltpu.VMEM((P_TOPK, tm_gate), jnp.float32), pltpu.VMEM((P_TOPK, tm_gate), jnp.float32)],
        compiler_params=pltpu.CompilerParams(dimension_semantics=("parallel",), **cparams),
        name="peer_gates",
    )(st)
    return pl.pallas_call(
        _peer_main_body,
        grid=(n // tm, n_exp // te),
        in_specs=[pl.BlockSpec((tm, d), lambda i, j: (i, 0)),
                  pl.BlockSpec((tm, d), lambda i, j: (i, 0)),
                  pl.BlockSpec((te, d), lambda i, j: (j, 0)),
                  pl.BlockSpec((d, te), lambda i, j: (0, j)),
                  pl.BlockSpec((P_HEADS, te // N_KEYS, tm), lambda i, j: (0, j, i)),
                  pl.BlockSpec((P_HEADS, N_KEYS, tm), lambda i, j: (0, 0, i)),
                  pl.BlockSpec((P_HEADS, tm), lambda i, j: (0, i))],
        out_specs=pl.BlockSpec((tm, d), lambda i, j: (i, 0)),
        out_shape=jax.ShapeDtypeStruct((n, d), jnp.float32),
        scratch_shapes=[pltpu.VMEM((d, tm), jnp.float32), pltpu.VMEM((te, tm), jnp.float32),
                        pltpu.VMEM((te, tm), jnp.bfloat16)],
        compiler_params=pltpu.CompilerParams(dimension_semantics=("parallel", "arbitrary"), **cparams),
        name="peer_main",
    )(y, xn, u_b, vt_b, e1, e2, gt)


def _final_norm_body(x_ref, g_ref, o_ref):
    x = x_ref[...]
    o_ref[...] = x * lax.rsqrt(jnp.mean(x * x, axis=-1, keepdims=True) + EPS) * g_ref[...]


def _final_norm(x, g, rows=512):
    n, d = x.shape
    return pl.pallas_call(
        _final_norm_body,
        grid=(n // rows,),
        in_specs=[pl.BlockSpec((rows, d), lambda i: (i, 0)), pl.BlockSpec((1, d), lambda i: (0, 0))],
        out_specs=pl.BlockSpec((rows, d), lambda i: (i, 0)),
        out_shape=jax.ShapeDtypeStruct((n, d), x.dtype),
        name="final_norm",
    )(x, g.reshape(1, d))


def kernel(x_prompt, x_sample, cache_diff, cache_nsa, state_nsa_win, cache_sb, page_table,
           norm_mix, norm_ffn, w_in_even, w_out_even, diff_lambda, diff_subln, nsa_cmp_pe, nsa_cmp_w,
           w_in_odd, w_out_odd, peer_wq, peer_subkeys, peer_u, peer_v, norm_final):
    bp, tp, d = x_prompt.shape
    bs, ts, _ = x_sample.shape
    n_p = bp * tp
    y_all = jnp.concatenate([x_prompt.reshape(n_p, d), x_sample.reshape(bs * ts, d)], axis=0)
    diff_p, diff_s, nsa_p, nsa_s, win_p, win_s, sb_p, sb_s = [], [], [], [], [], [], [], []
    bf16 = jnp.bfloat16
    for i in range(DEPTH):
        j = i // 2
        yp = y_all[:n_p].reshape(bp, tp, d)
        ys = y_all[n_p:].reshape(bs, ts, d)
        hp = rmsnorm(yp, norm_mix[i])
        hs = rmsnorm(ys, norm_mix[i])
        if i % 2 == 0:
            lam_init = 0.8 - 0.6 * math.exp(-0.3 * i)
            mp, r_diff, r_nsa, r_win = even_mixer_prompt(
                hp, w_in_even[j], w_out_even[j], diff_lambda[j], diff_subln[j], nsa_cmp_pe[j], nsa_cmp_w[j], lam_init)
            ms, s_diff, s_nsa, s_win = even_mixer_sample(
                hs, cache_diff, cache_nsa, state_nsa_win[j], page_table, j,
                w_in_even[j], w_out_even[j], diff_lambda[j], diff_subln[j], nsa_cmp_pe[j], nsa_cmp_w[j], lam_init)
            diff_p.append(r_diff)
            diff_s.append(s_diff)
            nsa_p.append(r_nsa)
            nsa_s.append(s_nsa)
            win_p.append(r_win)
            win_s.append(s_win)
        else:
            mp, r_sb = odd_mixer_prompt(hp, w_in_odd[j], w_out_odd[j])
            ms, s_sb = odd_mixer_sample(hs, cache_sb, page_table, j, w_in_odd[j], w_out_odd[j])
            sb_p.append(r_sb)
            sb_s.append(s_sb)
        y_all = y_all + jnp.concatenate([mp.reshape(n_p, d), ms.reshape(bs * ts, d)], axis=0)
        y_all = peer_residual(
            y_all, norm_ffn[i], peer_wq[i].astype(bf16),
            peer_subkeys[i].reshape(2 * P_HEADS, N_KEYS, D_KEY // 2).astype(bf16),
            peer_u[i].astype(bf16), peer_v[i].T.astype(bf16))
    y_out = _final_norm(y_all, norm_final)
    y_prompt = y_out[:n_p].reshape(bp, tp, d)
    y_sample = y_out[n_p:].reshape(bs, ts, d)
    return (y_prompt, y_sample, jnp.stack(diff_p), jnp.stack(diff_s), jnp.stack(nsa_p), jnp.stack(nsa_s),
            jnp.stack(win_p), jnp.stack(win_s), jnp.stack(sb_p), jnp.stack(sb_s))
```
